```python
import math
import jax, jax.numpy as jnp
from jax import lax
import numpy as np

D_MODEL = 2048
BATCH = 1
SEQ = 8192
DEPTH = 4

HEAD_DIM = 128
N_MIX_HEADS = D_MODEL // HEAD_DIM
A_HEADS = 3 * N_MIX_HEADS // 4
A_WIDTH = A_HEADS * HEAD_DIM
B_GROUPS = N_MIX_HEADS - A_HEADS
B_GROUP_DIM = HEAD_DIM
B_WIDTH = B_GROUPS * B_GROUP_DIM
EVEN_IN_WIDTH = 3 * A_WIDTH + B_WIDTH
DILATED_CONFIGS = ((128, 1), (512, 4), (2048, 16))
C_HEADS = D_MODEL // (2 * HEAD_DIM)
C_QK_WIDTH = C_HEADS * 2 * HEAD_DIM
C_V_DIM = 2 * HEAD_DIM
ODD_IN_WIDTH = 2 * C_QK_WIDTH + C_HEADS * C_V_DIM
D_FF = 4 * D_MODEL
NUM_BUCKETS = 32
MAX_DISTANCE = 1024
BIAS_HEADS = A_HEADS + C_HEADS
Q_BLOCK = 128
NORM_EPS = 1e-6
NEG_INF = -1e30
N_EVEN = (DEPTH + 1) // 2
N_ODD = DEPTH // 2

kernel_name = "hybrid_dilated_fourier_diffattn_encoder"


def rmsnorm(x, g):
    xf = x.astype(jnp.float32)
    y = xf * lax.rsqrt(jnp.mean(xf * xf, axis=-1, keepdims=True) + NORM_EPS)
    return (y * g.astype(jnp.float32)).astype(x.dtype)


def t5_bucket(rel):
    half = NUM_BUCKETS // 2
    base = jnp.where(rel > 0, half, 0)
    n = jnp.abs(rel)
    max_exact = half // 2
    nf = jnp.maximum(n, 1).astype(jnp.float32)
    large = max_exact + (jnp.log(nf / max_exact) / math.log(MAX_DISTANCE / max_exact)
                         * (half - max_exact)).astype(jnp.int32)
    large = jnp.minimum(large, half - 1)
    return base + jnp.where(n < max_exact, n, large)


def dilated_branch(q, k, v, table_a, window, dil):
    B, S, H, Dh = q.shape
    half = window // (2 * dil)
    qb = half
    L = S // dil
    nb = -(-L // qb)
    Lp = nb * qb

    def to_cls(t):
        t = t.reshape(B, L, dil, H, Dh).transpose(0, 2, 1, 3, 4)
        return jnp.pad(t, ((0, 0), (0, 0), (0, Lp - L), (0, 0), (0, 0)))

    def band(t):
        tp = jnp.pad(t, ((0, 0), (0, 0), (qb, qb), (0, 0), (0, 0)))
        tp = tp.reshape(B, dil, nb + 2, qb, H, Dh)
        return jnp.concatenate([tp[:, :, :-2], tp[:, :, 1:-1], tp[:, :, 2:]], axis=3)

    qblk = to_cls(q).reshape(B, dil, nb, qb, H, Dh)
    kband = band(to_cls(k))
    vband = band(to_cls(v)).astype(jnp.float32)

    qi = jnp.arange(qb)
    kj = jnp.arange(3 * qb)
    rel = kj[None, :] - qb - qi[:, None]
    t_key = jnp.arange(nb)[:, None] * qb + kj[None, :] - qb
    key_ok = (t_key >= 0) & (t_key < L)
    mask = (jnp.abs(rel) <= half)[None] & key_ok[:, None, :]
    bias = table_a.astype(jnp.float32)[t5_bucket(rel * dil)].transpose(2, 0, 1)

    logits = jnp.einsum('bcnqhd,bcnkhd->bcnhqk', qblk, kband).astype(jnp.float32)
    logits = logits * (1.0 / math.sqrt(Dh)) + bias[None, None, None]
    logits = jnp.where(mask[None, None, :, None], logits, NEG_INF)
    m = jnp.max(logits, axis=-1, keepdims=True)
    p = jnp.exp(logits - m)
    den = jnp.sum(p, axis=-1)
    o = jnp.einsum('bcnhqk,bcnkhd->bcnqhd', p, vband)
    o = o / den.transpose(0, 1, 2, 4, 3)[..., None]
    lse = (m[..., 0] + jnp.log(den)).transpose(0, 1, 2, 4, 3)

    o = o.reshape(B, dil, Lp, H, Dh)[:, :, :L].transpose(0, 2, 1, 3, 4).reshape(B, S, H, Dh)
    lse = lse.reshape(B, dil, Lp, H)[:, :, :L].transpose(0, 2, 1, 3).reshape(B, S, H)
    return o, lse


def dilated_attention(q, k, v, table_a):
    outs, lses = [], []
    for window, dil in DILATED_CONFIGS:
        o, l = dilated_branch(q, k, v, table_a, window, dil)
        outs.append(o)
        lses.append(l)
    w = jax.nn.softmax(jnp.stack(lses, axis=0), axis=0)
    return jnp.sum(w[..., None] * jnp.stack(outs, axis=0), axis=0)


def fourier_mix(u, w_f):
    B, S, _ = u.shape
    g = u.reshape(B, S, B_GROUPS, B_GROUP_DIM).astype(jnp.float32)
    f = jnp.fft.fft2(g, axes=(1, 3), norm='ortho').real
    return jnp.einsum('bsgc,gce->bsge', f, w_f.astype(jnp.float32)).reshape(B, S, B_WIDTH)


def even_mixer(h, w_in, w_f, w_out, table):
    B, S, _ = h.shape
    proj = h @ w_in
    qa = proj[..., :A_WIDTH].reshape(B, S, A_HEADS, HEAD_DIM)
    ka = proj[..., A_WIDTH:2 * A_WIDTH].reshape(B, S, A_HEADS, HEAD_DIM)
    va = proj[..., 2 * A_WIDTH:3 * A_WIDTH].reshape(B, S, A_HEADS, HEAD_DIM)
    ub = proj[..., 3 * A_WIDTH:]
    oa = dilated_attention(qa, ka, va, table[:, :A_HEADS]).reshape(B, S, A_WIDTH)
    ob = fourier_mix(ub, w_f)
    return jnp.concatenate([oa, ob], axis=-1).astype(h.dtype) @ w_out


def diff_attention(h, w_qkv, w_out, lq1, lk1, lq2, lk2, subln_g, table, layer_idx):
    B, S, _ = h.shape
    lambda_init = 0.8 - 0.6 * math.exp(-0.3 * layer_idx)
    proj = h @ w_qkv
    q = proj[..., :C_QK_WIDTH].reshape(B, S, C_HEADS, 2, HEAD_DIM)
    k = proj[..., C_QK_WIDTH:2 * C_QK_WIDTH].reshape(B, S, C_HEADS, 2, HEAD_DIM)
    v = proj[..., 2 * C_QK_WIDTH:].reshape(B, S, C_HEADS, C_V_DIM).astype(jnp.float32)
    f32 = jnp.float32
    lam = (jnp.exp(jnp.sum(lq1.astype(f32) * lk1.astype(f32)))
           - jnp.exp(jnp.sum(lq2.astype(f32) * lk2.astype(f32))) + lambda_init)
    table_c = table[:, A_HEADS:].astype(f32)
    nqb = S // Q_BLOCK
    qblocks = jnp.moveaxis(q.reshape(B, nqb, Q_BLOCK, C_HEADS, 2, HEAD_DIM), 1, 0)
    kpos = jnp.arange(S)
    scale = 1.0 / math.sqrt(HEAD_DIM)

    def one_block(args):
        qb, i = args
        qpos = i * Q_BLOCK + jnp.arange(Q_BLOCK)
        rel = kpos[None, :] - qpos[:, None]
        bias = table_c[t5_bucket(rel)].transpose(2, 0, 1)
        logits = jnp.einsum('bqhcd,bkhcd->bhcqk', qb, k).astype(f32) * scale
        p = jax.nn.softmax(logits + bias[None, :, None], axis=-1)
        a = p[:, :, 0] - lam * p[:, :, 1]
        return jnp.einsum('bhqk,bkhe->bqhe', a, v)

    o = lax.map(one_block, (qblocks, jnp.arange(nqb)))
    o = jnp.moveaxis(o, 0, 1).reshape(B, S, C_HEADS, C_V_DIM)
    o = rmsnorm(o, subln_g) * (1.0 - lambda_init)
    return o.reshape(B, S, C_HEADS * C_V_DIM).astype(h.dtype) @ w_out


def sqrelu_mlp(h, w1, w2):
    return jnp.square(jax.nn.relu(h @ w1)) @ w2


def setup_inputs(seed: int = 0) -> dict:
    key = jax.random.key(seed)
    ks = jax.random.split(key, 20)
    f32 = jnp.float32
    nrm = lambda k, shape, s: jax.random.normal(k, shape, f32) * s
    return {
        'x': nrm(ks[0], (BATCH, SEQ, D_MODEL), 1.0),
        'norm_mix_g': 1.0 + nrm(ks[1], (DEPTH, D_MODEL), 0.02),
        'norm_ffn_g': 1.0 + nrm(ks[2], (DEPTH, D_MODEL), 0.02),
        'norm_final_g': 1.0 + nrm(ks[3], (D_MODEL,), 0.02),
        'rel_bias_table': nrm(ks[4], (NUM_BUCKETS, BIAS_HEADS), 0.3),
        'w_in_even': nrm(ks[5], (N_EVEN, D_MODEL, EVEN_IN_WIDTH), D_MODEL ** -0.5),
        'w_fnet': nrm(ks[6], (N_EVEN, B_GROUPS, B_GROUP_DIM, B_GROUP_DIM), B_GROUP_DIM ** -0.5),
        'w_out_even': nrm(ks[7], (N_EVEN, D_MODEL, D_MODEL), D_MODEL ** -0.5),
        'w_qkv_odd': nrm(ks[8], (N_ODD, D_MODEL, ODD_IN_WIDTH), D_MODEL ** -0.5),
        'lambda_q1': nrm(ks[9], (N_ODD, HEAD_DIM), 0.1),
        'lambda_k1': nrm(ks[10], (N_ODD, HEAD_DIM), 0.1),
        'lambda_q2': nrm(ks[11], (N_ODD, HEAD_DIM), 0.1),
        'lambda_k2': nrm(ks[12], (N_ODD, HEAD_DIM), 0.1),
        'subln_g': 1.0 + nrm(ks[13], (N_ODD, C_V_DIM), 0.02),
        'w_out_odd': nrm(ks[14], (N_ODD, D_MODEL, D_MODEL), D_MODEL ** -0.5),
        'w_ff1': nrm(ks[15], (DEPTH, D_MODEL, D_FF), D_MODEL ** -0.5),
        'w_ff2': nrm(ks[16], (DEPTH, D_FF, D_MODEL), 0.5 * D_FF ** -0.5),
    }


def reference(x, norm_mix_g, norm_ffn_g, norm_final_g, rel_bias_table, w_in_even, w_fnet,
              w_out_even, w_qkv_odd, lambda_q1, lambda_k1, lambda_q2, lambda_k2, subln_g,
              w_out_odd, w_ff1, w_ff2):
    for i in range(DEPTH):
        h = rmsnorm(x, norm_mix_g[i])
        j = i // 2
        if i % 2 == 0:
            x = x + even_mixer(h, w_in_even[j], w_fnet[j], w_out_even[j], rel_bias_table)
        else:
            x = x + diff_attention(h, w_qkv_odd[j], w_out_odd[j], lambda_q1[j], lambda_k1[j],
                                   lambda_q2[j], lambda_k2[j], subln_g[j], rel_bias_table, i)
        x = x + sqrelu_mlp(rmsnorm(x, norm_ffn_g[i]), w_ff1[i], w_ff2[i])
    return rmsnorm(x, norm_final_g)
```

```python
import functools
import math

import numpy as np
import jax
import jax.numpy as jnp
from jax import lax
from jax.experimental import pallas as pl
from jax.experimental.pallas import tpu as pltpu

F32 = jnp.float32
BF16 = jnp.bfloat16

HEAD_DIM = 128
A_HEADS = 12
B_GROUPS = 4
C_HEADS = 8
DILATED_CONFIGS = ((128, 1), (512, 4), (2048, 16))
NUM_BUCKETS = 32
MAX_DISTANCE = 1024
NORM_EPS = 1e-6
NEG_INF = -1e30

A_WIDTH = A_HEADS * HEAD_DIM
B_WIDTH = B_GROUPS * HEAD_DIM
C_V_DIM = 2 * HEAD_DIM
HALF_STEPS = 64
assert all(w // (2 * d) == HALF_STEPS for w, d in DILATED_CONFIGS)

LANES = 128
V7X_VMEM_LIMIT_BYTES = 56 * 1024 * 1024

DIL_TQ = 512
DIL_SUB = 128
DIL_BAND = DIL_SUB + 2 * HALF_STEPS
DIL_HG = 4
DIFF_T = 512
DIFF_R = 3
FFT_N2 = 128


def _params(*sem):
    return pltpu.CompilerParams(dimension_semantics=sem, vmem_limit_bytes=V7X_VMEM_LIMIT_BYTES)


def _magnitude_thresholds():
    half = NUM_BUCKETS // 2
    max_exact = half // 2
    n = np.arange(1, 4 * MAX_DISTANCE, dtype=np.int64)
    large = max_exact + (np.log(n / max_exact) / math.log(MAX_DISTANCE / max_exact)
                         * (half - max_exact)).astype(np.int64)
    large = np.minimum(large, half - 1)
    bucket = np.where(n < max_exact, n, large)
    assert np.all(np.diff(bucket) >= 0)
    thr = {b: int(n[np.argmax(bucket >= b)]) for b in range(1, half)}
    return thr


_THR = _magnitude_thresholds()
_HALF_BUCKETS = NUM_BUCKETS // 2


def _mag_bucket(n):
    return sum(1 for b in range(1, _HALF_BUCKETS) if n >= _THR[b])


def _bias_chain(rel, lo, hi, tab):
    n = jnp.abs(rel)

    def side(base, nlo, nhi):
        bmin, bmax = _mag_bucket(nlo), _mag_bucket(nhi)
        val = jnp.full(rel.shape, tab(base + bmax), F32)
        for b in range(bmax - 1, bmin - 1, -1):
            val = jnp.where(n < _THR[b + 1], tab(base + b), val)
        return val

    if lo > 0:
        return side(_HALF_BUCKETS, lo, hi)
    if hi <= 0:
        return side(0, -hi, -lo)
    return jnp.where(rel > 0, side(_HALF_BUCKETS, 1, hi), side(0, 0, -lo))


def _rms(xf, g):
    ms = jnp.mean(xf * xf, axis=-1, keepdims=True)
    return xf * lax.rsqrt(ms + NORM_EPS) * g


def _dot(a, b):
    return jnp.dot(a, b, preferred_element_type=F32)


def _dot_nt(a, b):
    return lax.dot_general(a, b, (((1,), (1,)), ((), ())), preferred_element_type=F32)


def _norm_matmul_kernel(x_ref, g_ref, w_ref, o_ref, hn_ref):
    @pl.when(pl.program_id(1) == 0)
    def _():
        hn_ref[...] = _rms(x_ref[...], g_ref[...]).astype(BF16)

    o_ref[...] = _dot(hn_ref[...], w_ref[...]).astype(o_ref.dtype)


def _norm_matmul(x, g, w, *, tm=1024, tn=512):
    m, d = x.shape
    n = w.shape[1]
    tm = min(tm, m)
    return pl.pallas_call(
        _norm_matmul_kernel,
        grid=(m // tm, n // tn),
        in_specs=[pl.BlockSpec((tm, d), lambda i, j: (i, 0)),
                  pl.BlockSpec((1, d), lambda i, j: (0, 0)),
                  pl.BlockSpec((d, tn), lambda i, j: (0, j))],
        out_specs=pl.BlockSpec((tm, tn), lambda i, j: (i, j)),
        out_shape=jax.ShapeDtypeStruct((m, n), BF16),
        scratch_shapes=[pltpu.VMEM((tm, d), BF16)],
        compiler_params=_params("parallel", "arbitrary"),
        name="norm_matmul",
    )(x, g.reshape(1, d), w)


def _matmul_residual_kernel(a_ref, w_ref, r_ref, o_ref):
    o_ref[...] = r_ref[...] + _dot(a_ref[...], w_ref[...])


def _matmul_residual(a, w, res, *, tm=1024, tn=512):
    m, k = a.shape
    n = w.shape[1]
    tm = min(tm, m)
    return pl.pallas_call(
        _matmul_residual_kernel,
        grid=(m // tm, n // tn),
        in_specs=[pl.BlockSpec((tm, k), lambda i, j: (i, 0)),
                  pl.BlockSpec((k, tn), lambda i, j: (0, j)),
                  pl.BlockSpec((tm, tn), lambda i, j: (i, j))],
        out_specs=pl.BlockSpec((tm, tn), lambda i, j: (i, j)),
        out_shape=jax.ShapeDtypeStruct((m, n), F32),
        compiler_params=_params("parallel", "arbitrary"),
        name="matmul_residual",
    )(a, w, res)


def _mlp_kernel(x_ref, g_ref, w1_ref, w2_ref, gf_ref, o_ref, hn_ref, *, final_norm):
    c = pl.program_id(1)

    @pl.when(c == 0)
    def _():
        x = x_ref[...]
        hn_ref[...] = _rms(x, g_ref[...]).astype(BF16)
        o_ref[...] = x

    u = jnp.maximum(_dot(hn_ref[...], w1_ref[...]), 0.0)
    o_ref[...] += _dot((u * u).astype(BF16), w2_ref[...])

    if final_norm:
        @pl.when(c == pl.num_programs(1) - 1)
        def _():
            o_ref[...] = _rms(o_ref[...], gf_ref[...])


def _mlp(x, g, w1, w2, gf, *, final_norm, tm=512, tf=512):
    m, d = x.shape
    f = w1.shape[1]
    return pl.pallas_call(
        functools.partial(_mlp_kernel, final_norm=final_norm),
        grid=(m // tm, f // tf),
        in_specs=[pl.BlockSpec((tm, d), lambda i, c: (i, 0)),
                  pl.BlockSpec((1, d), lambda i, c: (0, 0)),
                  pl.BlockSpec((d, tf), lambda i, c: (0, c)),
                  pl.BlockSpec((tf, d), lambda i, c: (c, 0)),
                  pl.BlockSpec((1, d), lambda i, c: (0, 0))],
        out_specs=pl.BlockSpec((tm, d), lambda i, c: (i, 0)),
        out_shape=jax.ShapeDtypeStruct((m, d), F32),
        scratch_shapes=[pltpu.VMEM((tm, d), BF16)],
        compiler_params=_params("parallel", "arbitrary"),
        name="mlp",
    )(x, g.reshape(1, d), w1, w2, gf.reshape(1, d))


def _dilated_bias_kernel(tab_ref, o_ref):
    h = pl.program_id(0)
    row = lax.broadcasted_iota(jnp.int32, (DIL_SUB, DIL_BAND), 0)
    col = lax.broadcasted_iota(jnp.int32, (DIL_SUB, DIL_BAND), 1)
    steps = col - row - HALF_STEPS
    in_band = jnp.abs(steps) <= HALF_STEPS
    for bi, (_, dil) in enumerate(DILATED_CONFIGS):
        reach = HALF_STEPS * dil
        rel = jnp.clip(steps, -HALF_STEPS, HALF_STEPS) * dil
        bias = _bias_chain(rel, -reach, reach, lambda b: tab_ref[b, h])
        o_ref[bi, 0] = jnp.where(in_band, bias, NEG_INF)


def _dilated_bias(table):
    nb = len(DILATED_CONFIGS)
    return pl.pallas_call(
        _dilated_bias_kernel,
        grid=(A_HEADS,),
        in_specs=[pl.BlockSpec(memory_space=pltpu.SMEM)],
        out_specs=pl.BlockSpec((nb, 1, DIL_SUB, DIL_BAND), lambda h: (0, h, 0, 0)),
        out_shape=jax.ShapeDtypeStruct((nb, A_HEADS, DIL_SUB, DIL_BAND), F32),
        compiler_params=_params("arbitrary"),
        name="dilated_bias",
    )(table)


def _dilated_branch_kernel(q_ref, kp_ref, kc_ref, kn_ref, vp_ref, vc_ref, vn_ref, b_ref,
                           o_ref, lse_ref, *, tq):
    tb = pl.program_id(2)
    last_tb = pl.num_programs(2) - 1
    nsub = tq // DIL_SUB
    scale = 1.0 / math.sqrt(HEAD_DIM)
    col = lax.broadcasted_iota(jnp.int32, (DIL_SUB, DIL_BAND), 1)
    lane = lax.broadcasted_iota(jnp.int32, (DIL_SUB, LANES), 1)
    lanes_per_head = LANES // DIL_HG

    def band(prev_ref, cur_ref, next_ref, i, hs):
        lo = DIL_SUB * i - HALF_STEPS
        hi = lo + DIL_BAND
        parts = []
        if lo < 0:
            parts.append(prev_ref[:, hs])
        parts.append(cur_ref[max(lo, 0):min(hi, tq), hs])
        if hi > tq:
            parts.append(next_ref[:, hs])
        return parts[0] if len(parts) == 1 else jnp.concatenate(parts, axis=0)

    for i in range(nsub):
        lse_blk = None
        for j in range(DIL_HG):
            hs = slice(HEAD_DIM * j, HEAD_DIM * (j + 1))
            q = q_ref[DIL_SUB * i:DIL_SUB * (i + 1), hs]
            kb = band(kp_ref, kc_ref, kn_ref, i, hs)
            vb = band(vp_ref, vc_ref, vn_ref, i, hs)
            s = _dot_nt(q, kb) * scale + b_ref[j]
            if i == 0:
                s = jnp.where((tb > 0) | (col >= HALF_STEPS), s, NEG_INF)
            if i == nsub - 1:
                s = jnp.where((tb < last_tb) | (col < DIL_BAND - HALF_STEPS), s, NEG_INF)
            m = jnp.max(s, axis=-1, keepdims=True)
            p = jnp.exp(s - m)
            den = jnp.sum(p, axis=-1, keepdims=True)
            o = _dot(p.astype(BF16), vb) / den
            o_ref[DIL_SUB * i:DIL_SUB * (i + 1), hs] = o.astype(o_ref.dtype)
            lse = m + jnp.log(den)
            lse_blk = lse if lse_blk is None else jnp.where(lane >= lanes_per_head * j, lse, lse_blk)
        lse_ref[DIL_SUB * i:DIL_SUB * (i + 1), :] = jnp.broadcast_to(lse_blk, (DIL_SUB, LANES))


def _dilated_branch(proj, bias_b, dil, *, tq=DIL_TQ):
    s, w = proj.shape
    l = s // dil
    tq = min(tq, l)
    ngroups = A_HEADS // DIL_HG
    gw = DIL_HG * HEAD_DIM
    wb = w // gw
    qoff, koff, voff = 0, A_WIDTH // gw, 2 * A_WIDTH // gw
    halo_per_tq = tq // HALF_STEPS
    n_halo = l // HALF_STEPS
    view = proj.reshape(l, dil * w)

    def cur(off):
        return pl.BlockSpec((tq, gw), lambda r, g, t: (t, r * wb + off + g))

    def prev(off):
        return pl.BlockSpec((HALF_STEPS, gw),
                            lambda r, g, t: (jnp.maximum(t * halo_per_tq - 1, 0), r * wb + off + g))

    def nxt(off):
        return pl.BlockSpec((HALF_STEPS, gw),
                            lambda r, g, t: (jnp.minimum((t + 1) * halo_per_tq, n_halo - 1), r * wb + off + g))

    o, lse = pl.pallas_call(
        functools.partial(_dilated_branch_kernel, tq=tq),
        grid=(dil, ngroups, l // tq),
        in_specs=[cur(qoff), prev(koff), cur(koff), nxt(koff), prev(voff), cur(voff), nxt(voff),
                  pl.BlockSpec((DIL_HG, DIL_SUB, DIL_BAND), lambda r, g, t: (g, 0, 0))],
        out_specs=[pl.BlockSpec((tq, gw), lambda r, g, t: (t, r * ngroups + g)),
                   pl.BlockSpec((tq, LANES), lambda r, g, t: (t, r * ngroups + g))],
        out_shape=[jax.ShapeDtypeStruct((l, dil * A_WIDTH), BF16),
                   jax.ShapeDtypeStruct((l, dil * ngroups * LANES), F32)],
        compiler_params=_params("parallel", "parallel", "arbitrary"),
        name=f"dilated_branch_d{dil}",
    )(view, view, view, view, view, view, view, bias_b)
    return o.reshape(s, A_WIDTH), lse.reshape(s, ngroups * LANES)


def _dilated_merge_kernel(o1_ref, o2_ref, o3_ref, l1_ref, l2_ref, l3_ref, fb_ref, out_ref):
    g = pl.program_id(1)
    ngroups = A_HEADS // DIL_HG
    lanes_per_head = LANES // DIL_HG

    @pl.when(g < ngroups)
    def _():
        ls = [l1_ref[...], l2_ref[...], l3_ref[...]]
        mx = jnp.maximum(jnp.maximum(ls[0], ls[1]), ls[2])
        es = [jnp.exp(l - mx) for l in ls]
        inv = 1.0 / (es[0] + es[1] + es[2])
        ws = [e * inv for e in es]
        for j in range(DIL_HG):
            hs = slice(HEAD_DIM * j, HEAD_DIM * (j + 1))
            lc = slice(lanes_per_head * j, lanes_per_head * j + 1)
            acc = ws[0][:, lc] * o1_ref[:, hs].astype(F32)
            acc += ws[1][:, lc] * o2_ref[:, hs].astype(F32)
            acc += ws[2][:, lc] * o3_ref[:, hs].astype(F32)
            out_ref[:, hs] = acc.astype(out_ref.dtype)

    @pl.when(g == ngroups)
    def _():
        out_ref[...] = fb_ref[...]


def _dilated_merge(os_, lses, fb, *, tm=512):
    s = fb.shape[0]
    ngroups = A_HEADS // DIL_HG
    gw = DIL_HG * HEAD_DIM
    assert fb.shape[1] == gw
    clamp = lambda i, g: (i, jnp.minimum(g, ngroups - 1))
    return pl.pallas_call(
        _dilated_merge_kernel,
        grid=(s // tm, ngroups + 1),
        in_specs=[pl.BlockSpec((tm, gw), clamp)] * 3 + [pl.BlockSpec((tm, LANES), clamp)] * 3
                 + [pl.BlockSpec((tm, gw), lambda i, g: (i, 0))],
        out_specs=pl.BlockSpec((tm, gw), lambda i, g: (i, g)),
        out_shape=jax.ShapeDtypeStruct((s, A_WIDTH + B_WIDTH), BF16),
        compiler_params=_params("parallel", "arbitrary"),
        name="dilated_merge",
    )(*os_, *lses, fb)


def _dft_cos_sin(n):
    idx = np.arange(n, dtype=np.int64)
    ang = 2.0 * np.pi * ((idx[:, None] * idx[None, :]) % n) / n
    return np.cos(ang), np.sin(ang)


def _fourier_weights_kernel(cs_ref, w_ref, o_ref, *, norm):
    w = w_ref[0]
    ab = jnp.dot(cs_ref[...], w, preferred_element_type=F32, precision=lax.Precision.HIGHEST) * norm
    o_ref[0] = jnp.concatenate([ab[:HEAD_DIM], ab[HEAD_DIM:]], axis=1).astype(o_ref.dtype)


def _fourier_weights(w_f, seq):
    c, s = _dft_cos_sin(HEAD_DIM)
    cs = jnp.asarray(np.concatenate([c, s], axis=0), F32)
    norm = 1.0 / math.sqrt(seq * HEAD_DIM)
    return pl.pallas_call(
        functools.partial(_fourier_weights_kernel, norm=norm),
        grid=(B_GROUPS,),
        in_specs=[pl.BlockSpec((2 * HEAD_DIM, HEAD_DIM), lambda g: (0, 0)),
                  pl.BlockSpec((1, HEAD_DIM, HEAD_DIM), lambda g: (g, 0, 0))],
        out_specs=pl.BlockSpec((1, HEAD_DIM, 2 * HEAD_DIM), lambda g: (g, 0, 0)),
        out_shape=jax.ShapeDtypeStruct((B_GROUPS, HEAD_DIM, 2 * HEAD_DIM), BF16),
        compiler_params=_params("arbitrary"),
        name="fourier_weights",
    )(cs, w_f)


def _fourier_channel_kernel(u_ref, ab_ref, y_ref, z_ref):
    for g in range(B_GROUPS):
        hs = slice(HEAD_DIM * g, HEAD_DIM * (g + 1))
        yz = _dot(u_ref[:, hs], ab_ref[g])
        y_ref[:, hs] = yz[:, :HEAD_DIM].astype(y_ref.dtype)
        z_ref[:, hs] = yz[:, HEAD_DIM:].astype(z_ref.dtype)


def _fourier_channel(proj, ab, *, tm=1024):
    s, w = proj.shape
    ublock = (w - B_WIDTH) // B_WIDTH
    assert ublock * B_WIDTH == w - B_WIDTH
    spec = pl.BlockSpec((tm, B_WIDTH), lambda i: (i, 0))
    return pl.pallas_call(
        _fourier_channel_kernel,
        grid=(s // tm,),
        in_specs=[pl.BlockSpec((tm, B_WIDTH), lambda i: (i, ublock)),
                  pl.BlockSpec((B_GROUPS, HEAD_DIM, 2 * HEAD_DIM), lambda i: (0, 0, 0))],
        out_specs=[spec, spec],
        out_shape=[jax.ShapeDtypeStruct((s, B_WIDTH), BF16)] * 2,
        compiler_params=_params("parallel"),
        name="fourier_channel",
    )(proj, ab)


def _fourier_stage1_kernel(m1_ref, tc_ref, ts_ref, y_ref, z_ref, tre_ref, tim_ref, *, n1, n2_per_step):
    yz = jnp.concatenate([y_ref[...], z_ref[...]], axis=0)
    ab = _dot(m1_ref[...], yz)
    a, b = ab[:n1], ab[n1:]
    ch = B_WIDTH
    for q in range(n2_per_step):
        cs = slice(ch * q, ch * (q + 1))
        c = tc_ref[0, :, q:q + 1]
        s = ts_ref[0, :, q:q + 1]
        aq, bq = a[:, cs], b[:, cs]
        tre_ref[:, cs] = (aq * c + bq * s).astype(tre_ref.dtype)
        tim_ref[:, cs] = (bq * c - aq * s).astype(tim_ref.dtype)


def _fourier_stage2_kernel(m2_ref, tre_ref, tim_ref, o_ref, *, n2, k1_per_step):
    ch = B_WIDTH
    for q in range(k1_per_step):
        rs = slice(n2 * q, n2 * (q + 1))
        t = jnp.concatenate([tre_ref[rs, :], tim_ref[rs, :]], axis=0)
        o_ref[:, ch * q:ch * (q + 1)] = _dot(m2_ref[...], t).astype(o_ref.dtype)


def _fourier_position(y, z, *, n2=FFT_N2, n2_per_step=16, k1_per_step=4):
    seq, ch = y.shape
    n1 = seq // n2
    n2_per_step = min(n2_per_step, n2)
    k1_per_step = min(k1_per_step, n1)
    c1, s1 = _dft_cos_sin(n1)
    m1 = jnp.asarray(np.block([[c1, -s1], [-s1, -c1]]), BF16)
    c2, s2 = _dft_cos_sin(n2)
    m2 = jnp.asarray(np.concatenate([c2, s2], axis=1), BF16)
    k1 = np.arange(n1, dtype=np.int64)[:, None]
    nn2 = np.arange(n2, dtype=np.int64)[None, :]
    ang = 2.0 * np.pi * ((k1 * nn2) % seq) / seq
    steps = n2 // n2_per_step
    tc = jnp.asarray(np.cos(ang).reshape(n1, steps, n2_per_step).transpose(1, 0, 2), F32)
    ts = jnp.asarray(np.sin(ang).reshape(n1, steps, n2_per_step).transpose(1, 0, 2), F32)

    cols = n2_per_step * ch
    dspec = pl.BlockSpec((n1, cols), lambda t: (0, t))
    tspec = pl.BlockSpec((1, n1, n2_per_step), lambda t: (t, 0, 0))
    tre, tim = pl.pallas_call(
        functools.partial(_fourier_stage1_kernel, n1=n1, n2_per_step=n2_per_step),
        grid=(steps,),
        in_specs=[pl.BlockSpec((2 * n1, 2 * n1), lambda t: (0, 0)), tspec, tspec, dspec, dspec],
        out_specs=[dspec, dspec],
        out_shape=[jax.ShapeDtypeStruct((n1, n2 * ch), BF16)] * 2,
        compiler_params=_params("parallel"),
        name="fourier_stage1",
    )(m1, tc, ts, y.reshape(n1, n2 * ch), z.reshape(n1, n2 * ch))

    tblock = pl.BlockSpec((k1_per_step * n2, ch), lambda t: (t, 0))
    out = pl.pallas_call(
        functools.partial(_fourier_stage2_kernel, n2=n2, k1_per_step=k1_per_step),
        grid=(n1 // k1_per_step,),
        in_specs=[pl.BlockSpec((n2, 2 * n2), lambda t: (0, 0)), tblock, tblock],
        out_specs=pl.BlockSpec((n2, k1_per_step * ch), lambda t: (0, t)),
        out_shape=jax.ShapeDtypeStruct((n2, n1 * ch), BF16),
        compiler_params=_params("parallel"),
        name="fourier_stage2",
    )(m2, tre.reshape(seq, ch), tim.reshape(seq, ch))
    return out.reshape(seq, ch)


def _diff_bias_kernel(tab_ref, o_ref, *, t):
    h = pl.program_id(0)
    row = lax.broadcasted_iota(jnp.int32, (t, t), 0)
    col = lax.broadcasted_iota(jnp.int32, (t, t), 1)
    base = col - row
    for r in range(2 * DIFF_R + 1):
        d = (r - DIFF_R) * t
        o_ref[0, r] = _bias_chain(base + d, d - (t - 1), d + (t - 1), lambda b: tab_ref[b, A_HEADS + h])


def _diff_bias(table, t):
    assert DIFF_R * t - (t - 1) >= _THR[_HALF_BUCKETS - 1]
    nt = 2 * DIFF_R + 1
    return pl.pallas_call(
        functools.partial(_diff_bias_kernel, t=t),
        grid=(C_HEADS,),
        in_specs=[pl.BlockSpec(memory_space=pltpu.SMEM)],
        out_specs=pl.BlockSpec((1, nt, t, t), lambda h: (h, 0, 0, 0)),
        out_shape=jax.ShapeDtypeStruct((C_HEADS, nt, t, t), F32),
        compiler_params=_params("arbitrary"),
        name="diff_bias",
    )(table)


def _diff_attn_kernel(q_ref, k_ref, v_ref, b_ref, lam_ref, g_ref, o_ref, m_sc, l_sc, acc_sc,
                      *, t, lambda_init):
    qi = pl.program_id(1)
    nkv = k_ref.shape[0] // t
    scale = 1.0 / math.sqrt(HEAD_DIM)
    nchunk = t // LANES

    m_sc[...] = jnp.full(m_sc.shape, NEG_INF, F32)
    l_sc[...] = jnp.zeros(l_sc.shape, F32)
    acc_sc[...] = jnp.zeros(acc_sc.shape, F32)

    def body(j, carry):
        start = pl.multiple_of(j * t, t)
        kblk = k_ref[pl.ds(start, t), :]
        vblk = v_ref[pl.ds(start, t), :]
        bias = b_ref[0, jnp.clip(j - qi, -DIFF_R, DIFF_R) + DIFF_R]
        for c in range(2):
            hs = slice(HEAD_DIM * c, HEAD_DIM * (c + 1))
            s = _dot_nt(q_ref[:, hs], kblk[:, hs]) * scale + bias
            chunks = [s[:, LANES * i:LANES * (i + 1)] for i in range(nchunk)]
            cmax = functools.reduce(jnp.maximum, chunks)
            m_prev = m_sc[c]
            m_next = jnp.maximum(m_prev, jnp.max(cmax, axis=-1, keepdims=True))
            alpha = jnp.exp(m_prev - m_next)
            ps = [jnp.exp(ch - m_next) for ch in chunks]
            psum = functools.reduce(jnp.add, ps)
            l_sc[c] = alpha * l_sc[c] + jnp.sum(psum, axis=-1, keepdims=True)
            m_sc[c] = m_next
            p = jnp.concatenate(ps, axis=1).astype(BF16)
            acc_sc[c] = acc_sc[c] * jnp.concatenate([alpha, alpha], axis=1) + _dot(p, vblk)
        return carry

    lax.fori_loop(0, nkv, body, 0)

    lp = lam_ref[...]
    lam = (jnp.exp(jnp.sum(lp[0:1] * lp[1:2], axis=-1, keepdims=True))
           - jnp.exp(jnp.sum(lp[2:3] * lp[3:4], axis=-1, keepdims=True)) + lambda_init)
    inv0 = 1.0 / l_sc[0]
    inv1 = 1.0 / l_sc[1]
    o = (acc_sc[0] * jnp.concatenate([inv0, inv0], axis=1)
         - lam * (acc_sc[1] * jnp.concatenate([inv1, inv1], axis=1)))
    o_ref[...] = (_rms(o, g_ref[...]) * (1.0 - lambda_init)).astype(o_ref.dtype)


def _diff_attn(proj, bias, lam_params, subln_g, lambda_init, *, t):
    s = proj.shape[0]
    nt = 2 * DIFF_R + 1
    return pl.pallas_call(
        functools.partial(_diff_attn_kernel, t=t, lambda_init=lambda_init),
        grid=(C_HEADS, s // t),
        in_specs=[pl.BlockSpec((t, C_V_DIM), lambda h, i: (i, h)),
                  pl.BlockSpec((s, C_V_DIM), lambda h, i: (0, C_HEADS + h)),
                  pl.BlockSpec((s, C_V_DIM), lambda h, i: (0, 2 * C_HEADS + h)),
                  pl.BlockSpec((1, nt, t, t), lambda h, i: (h, 0, 0, 0)),
                  pl.BlockSpec((4, HEAD_DIM), lambda h, i: (0, 0)),
                  pl.BlockSpec((1, C_V_DIM), lambda h, i: (0, 0))],
        out_specs=pl.BlockSpec((t, C_V_DIM), lambda h, i: (i, h)),
        out_shape=jax.ShapeDtypeStruct((s, C_HEADS * C_V_DIM), BF16),
        scratch_shapes=[pltpu.VMEM((2, t, LANES), F32), pltpu.VMEM((2, t, LANES), F32),
                        pltpu.VMEM((2, t, C_V_DIM), F32)],
        compiler_params=_params("parallel", "arbitrary"),
        name="diff_attn",
    )(proj, proj, proj, bias, lam_params, subln_g.reshape(1, C_V_DIM))


def kernel(x, norm_mix_g, norm_ffn_g, norm_final_g, rel_bias_table, w_in_even, w_fnet, w_out_even,
           w_qkv_odd, lambda_q1, lambda_k1, lambda_q2, lambda_k2, subln_g, w_out_odd, w_ff1, w_ff2):
    batch, seq, d_model = x.shape
    depth = norm_mix_g.shape[0]
    table = rel_bias_table.astype(F32)
    dil_bias = _dilated_bias(table)
    diff_t = min(DIFF_T, seq)
    diff_bias = _diff_bias(table, diff_t)

    outs = []
    for bidx in range(batch):
        xs = x[bidx]
        for i in range(depth):
            j = i // 2
            if i % 2 == 0:
                proj = _norm_matmul(xs, norm_mix_g[i], w_in_even[j].astype(BF16))
                branches = [_dilated_branch(proj, dil_bias[bi], dil)
                            for bi, (_, dil) in enumerate(DILATED_CONFIGS)]
                y, z = _fourier_channel(proj, _fourier_weights(w_fnet[j], seq))
                fb = _fourier_position(y, z)
                mixed = _dilated_merge([b[0] for b in branches], [b[1] for b in branches], fb)
                xs = _matmul_residual(mixed, w_out_even[j].astype(BF16), xs)
            else:
                lambda_init = 0.8 - 0.6 * math.exp(-0.3 * i)
                proj = _norm_matmul(xs, norm_mix_g[i], w_qkv_odd[j].astype(BF16))
                lam_params = jnp.stack([lambda_q1[j], lambda_k1[j], lambda_q2[j], lambda_k2[j]]).astype(F32)
                attn = _diff_attn(proj, diff_bias, lam_params, subln_g[j], lambda_init, t=diff_t)
                xs = _matmul_residual(attn, w_out_odd[j].astype(BF16), xs)
            xs = _mlp(xs, norm_ffn_g[i], w_ff1[i].astype(BF16), w_ff2[i].astype(BF16), norm_final_g,
                      final_norm=(i == depth - 1))
        outs.append(xs)
    return jnp.stack(outs, axis=0)
```

```python
import functools
import math

import numpy as np
import jax
import jax.numpy as jnp
from jax import lax
from jax.experimental import pallas as pl
from jax.experimental.pallas import tpu as pltpu

F32 = jnp.float32
BF16 = jnp.bfloat16

HEAD_DIM = 128
A_HEADS = 12
B_GROUPS = 4
C_HEADS = 8
DILATED_CONFIGS = ((128, 1), (512, 4), (2048, 16))
NUM_BUCKETS = 32
MAX_DISTANCE = 1024
NORM_EPS = 1e-6
NEG_INF = -1e30

A_WIDTH = A_HEADS * HEAD_DIM
B_WIDTH = B_GROUPS * HEAD_DIM
C_V_DIM = 2 * HEAD_DIM
HALF_STEPS = 64
assert all(w // (2 * d) == HALF_STEPS for w, d in DILATED_CONFIGS)

LANES = 128
V7X_VMEM_LIMIT_BYTES = 56 * 1024 * 1024

DIL_TQ = 512
DIL_SUB = 128
DIL_BAND = DIL_SUB + 2 * HALF_STEPS
DIL_HG = 4
DIFF_T = 512
DIFF_R = 3
FFT_N2 = 128


def _params(*sem):
    return pltpu.CompilerParams(dimension_semantics=sem, vmem_limit_bytes=V7X_VMEM_LIMIT_BYTES)


def _magnitude_thresholds():
    half = NUM_BUCKETS // 2
    max_exact = half // 2
    n = np.arange(1, 4 * MAX_DISTANCE, dtype=np.int64)
    large = max_exact + (np.log(n / max_exact) / math.log(MAX_DISTANCE / max_exact)
                         * (half - max_exact)).astype(np.int64)
    large = np.minimum(large, half - 1)
    bucket = np.where(n < max_exact, n, large)
    assert np.all(np.diff(bucket) >= 0)
    thr = {b: int(n[np.argmax(bucket >= b)]) for b in range(1, half)}
    return thr


_THR = _magnitude_thresholds()
_HALF_BUCKETS = NUM_BUCKETS // 2


def _mag_bucket(n):
    return sum(1 for b in range(1, _HALF_BUCKETS) if n >= _THR[b])


def _bias_chain(rel, lo, hi, tab):
    n = jnp.abs(rel)

    def side(base, nlo, nhi):
        bmin, bmax = _mag_bucket(nlo), _mag_bucket(nhi)
        val = jnp.full(rel.shape, tab(base + bmax), F32)
        for b in range(bmax - 1, bmin - 1, -1):
            val = jnp.where(n < _THR[b + 1], tab(base + b), val)
        return val

    if lo > 0:
        return side(_HALF_BUCKETS, lo, hi)
    if hi <= 0:
        return side(0, -hi, -lo)
    return jnp.where(rel > 0, side(_HALF_BUCKETS, 1, hi), side(0, 0, -lo))


def _rms(xf, g):
    ms = jnp.mean(xf * xf, axis=-1, keepdims=True)
    return xf * lax.rsqrt(ms + NORM_EPS) * g


def _dot(a, b):
    return jnp.dot(a, b, preferred_element_type=F32)


def _dot_nt(a, b):
    return lax.dot_general(a, b, (((1,), (1,)), ((), ())), preferred_element_type=F32)


def _norm_matmul_kernel(x_ref, g_ref, w_ref, cs_ref, o_ref, hn_ref):
    @pl.when(pl.program_id(1) == 0)
    def _():
        hn_ref[...] = _rms(x_ref[...], g_ref[...]).astype(BF16)

    o_ref[...] = (_dot(hn_ref[...], w_ref[...]) * cs_ref[...]).astype(o_ref.dtype)


def _norm_matmul(x, g, w, col_scale, *, tm=1024, tn=512):
    m, d = x.shape
    n = w.shape[1]
    tm = min(tm, m)
    return pl.pallas_call(
        _norm_matmul_kernel,
        grid=(m // tm, n // tn),
        in_specs=[pl.BlockSpec((tm, d), lambda i, j: (i, 0)),
                  pl.BlockSpec((1, d), lambda i, j: (0, 0)),
                  pl.BlockSpec((d, tn), lambda i, j: (0, j)),
                  pl.BlockSpec((1, tn), lambda i, j: (0, j))],
        out_specs=pl.BlockSpec((tm, tn), lambda i, j: (i, j)),
        out_shape=jax.ShapeDtypeStruct((m, n), BF16),
        scratch_shapes=[pltpu.VMEM((tm, d), BF16)],
        compiler_params=_params("parallel", "arbitrary"),
        name="norm_matmul",
    )(x, g.reshape(1, d), w, col_scale)


def _matmul_residual_kernel(a_ref, w_ref, r_ref, o_ref):
    o_ref[...] = r_ref[...] + _dot(a_ref[...], w_ref[...])


def _matmul_residual(a, w, res, *, tm=1024, tn=512):
    m, k = a.shape
    n = w.shape[1]
    tm = min(tm, m)
    return pl.pallas_call(
        _matmul_residual_kernel,
        grid=(m // tm, n // tn),
        in_specs=[pl.BlockSpec((tm, k), lambda i, j: (i, 0)),
                  pl.BlockSpec((k, tn), lambda i, j: (0, j)),
                  pl.BlockSpec((tm, tn), lambda i, j: (i, j))],
        out_specs=pl.BlockSpec((tm, tn), lambda i, j: (i, j)),
        out_shape=jax.ShapeDtypeStruct((m, n), F32),
        compiler_params=_params("parallel", "arbitrary"),
        name="matmul_residual",
    )(a, w, res)


def _mlp_kernel(x_ref, g_ref, w1_ref, w2_ref, gf_ref, o_ref, hn_ref, *, final_norm):
    c = pl.program_id(1)

    @pl.when(c == 0)
    def _():
        x = x_ref[...]
        hn_ref[...] = _rms(x, g_ref[...]).astype(BF16)
        o_ref[...] = x

    u = jnp.maximum(_dot(hn_ref[...], w1_ref[...]), 0.0)
    o_ref[...] += _dot((u * u).astype(BF16), w2_ref[...])

    if final_norm:
        @pl.when(c == pl.num_programs(1) - 1)
        def _():
            o_ref[...] = _rms(o_ref[...], gf_ref[...])


def _mlp(x, g, w1, w2, gf, *, final_norm, tm=512, tf=512):
    m, d = x.shape
    f = w1.shape[1]
    return pl.pallas_call(
        functools.partial(_mlp_kernel, final_norm=final_norm),
        grid=(m // tm, f // tf),
        in_specs=[pl.BlockSpec((tm, d), lambda i, c: (i, 0)),
                  pl.BlockSpec((1, d), lambda i, c: (0, 0)),
                  pl.BlockSpec((d, tf), lambda i, c: (0, c)),
                  pl.BlockSpec((tf, d), lambda i, c: (c, 0)),
                  pl.BlockSpec((1, d), lambda i, c: (0, 0))],
        out_specs=pl.BlockSpec((tm, d), lambda i, c: (i, 0)),
        out_shape=jax.ShapeDtypeStruct((m, d), F32),
        scratch_shapes=[pltpu.VMEM((tm, d), BF16)],
        compiler_params=_params("parallel", "arbitrary"),
        name="mlp",
    )(x, g.reshape(1, d), w1, w2, gf.reshape(1, d))


def _dilated_bias_kernel(tab_ref, o_ref):
    h = pl.program_id(0)
    row = lax.broadcasted_iota(jnp.int32, (DIL_SUB, DIL_BAND), 0)
    col = lax.broadcasted_iota(jnp.int32, (DIL_SUB, DIL_BAND), 1)
    steps = col - row - HALF_STEPS
    in_band = jnp.abs(steps) <= HALF_STEPS
    for bi, (_, dil) in enumerate(DILATED_CONFIGS):
        reach = HALF_STEPS * dil
        rel = jnp.clip(steps, -HALF_STEPS, HALF_STEPS) * dil
        bias = _bias_chain(rel, -reach, reach, lambda b: tab_ref[b, h])
        o_ref[bi, 0] = jnp.where(in_band, bias, NEG_INF)


def _dilated_bias(table):
    nb = len(DILATED_CONFIGS)
    return pl.pallas_call(
        _dilated_bias_kernel,
        grid=(A_HEADS,),
        in_specs=[pl.BlockSpec(memory_space=pltpu.SMEM)],
        out_specs=pl.BlockSpec((nb, 1, DIL_SUB, DIL_BAND), lambda h: (0, h, 0, 0)),
        out_shape=jax.ShapeDtypeStruct((nb, A_HEADS, DIL_SUB, DIL_BAND), F32),
        compiler_params=_params("arbitrary"),
        name="dilated_bias",
    )(table)


def _dilated_branch_kernel(q_ref, kp_ref, kc_ref, kn_ref, vp_ref, vc_ref, vn_ref, b_ref,
                           o_ref, lse_ref, *, tq):
    tb = pl.program_id(2)
    last_tb = pl.num_programs(2) - 1
    nsub = tq // DIL_SUB
    scale = 1.0 / math.sqrt(HEAD_DIM)
    col = lax.broadcasted_iota(jnp.int32, (DIL_SUB, DIL_BAND), 1)
    lane = lax.broadcasted_iota(jnp.int32, (DIL_SUB, LANES), 1)
    lanes_per_head = LANES // DIL_HG

    def band(prev_ref, cur_ref, next_ref, i, hs):
        lo = DIL_SUB * i - HALF_STEPS
        hi = lo + DIL_BAND
        parts = []
        if lo < 0:
            parts.append(prev_ref[:, hs])
        parts.append(cur_ref[max(lo, 0):min(hi, tq), hs])
        if hi > tq:
            parts.append(next_ref[:, hs])
        return parts[0] if len(parts) == 1 else jnp.concatenate(parts, axis=0)

    for i in range(nsub):
        lse_blk = None
        for j in range(DIL_HG):
            hs = slice(HEAD_DIM * j, HEAD_DIM * (j + 1))
            q = q_ref[DIL_SUB * i:DIL_SUB * (i + 1), hs]
            kb = band(kp_ref, kc_ref, kn_ref, i, hs)
            vb = band(vp_ref, vc_ref, vn_ref, i, hs)
            s = _dot_nt(q, kb) * scale + b_ref[j]
            if i == 0:
                s = jnp.where((tb > 0) | (col >= HALF_STEPS), s, NEG_INF)
            if i == nsub - 1:
                s = jnp.where((tb < last_tb) | (col < DIL_BAND - HALF_STEPS), s, NEG_INF)
            m = jnp.max(s, axis=-1, keepdims=True)
            p = jnp.exp(s - m)
            den = jnp.sum(p, axis=-1, keepdims=True)
            o = _dot(p.astype(BF16), vb) / den
            o_ref[DIL_SUB * i:DIL_SUB * (i + 1), hs] = o.astype(o_ref.dtype)
            lse = m + jnp.log(den)
            lse_blk = lse if lse_blk is None else jnp.where(lane >= lanes_per_head * j, lse, lse_blk)
        lse_ref[DIL_SUB * i:DIL_SUB * (i + 1), :] = jnp.broadcast_to(lse_blk, (DIL_SUB, LANES))


def _dilated_branch(proj, bias_b, dil, *, tq=DIL_TQ):
    s, w = proj.shape
    l = s // dil
    tq = min(tq, l)
    ngroups = A_HEADS // DIL_HG
    gw = DIL_HG * HEAD_DIM
    wb = w // gw
    qoff, koff, voff = 0, A_WIDTH // gw, 2 * A_WIDTH // gw
    halo_per_tq = tq // HALF_STEPS
    n_halo = l // HALF_STEPS
    view = proj.reshape(l, dil * w)

    def cur(off):
        return pl.BlockSpec((tq, gw), lambda r, g, t: (t, r * wb + off + g))

    def prev(off):
        return pl.BlockSpec((HALF_STEPS, gw),
                            lambda r, g, t: (jnp.maximum(t * halo_per_tq - 1, 0), r * wb + off + g))

    def nxt(off):
        return pl.BlockSpec((HALF_STEPS, gw),
                            lambda r, g, t: (jnp.minimum((t + 1) * halo_per_tq, n_halo - 1), r * wb + off + g))

    o, lse = pl.pallas_call(
        functools.partial(_dilated_branch_kernel, tq=tq),
        grid=(dil, ngroups, l // tq),
        in_specs=[cur(qoff), prev(koff), cur(koff), nxt(koff), prev(voff), cur(voff), nxt(voff),
                  pl.BlockSpec((DIL_HG, DIL_SUB, DIL_BAND), lambda r, g, t: (g, 0, 0))],
        out_specs=[pl.BlockSpec((tq, gw), lambda r, g, t: (t, r * ngroups + g)),
                   pl.BlockSpec((tq, LANES), lambda r, g, t: (t, r * ngroups + g))],
        out_shape=[jax.ShapeDtypeStruct((l, dil * A_WIDTH), BF16),
                   jax.ShapeDtypeStruct((l, dil * ngroups * LANES), F32)],
        compiler_params=_params("parallel", "parallel", "arbitrary"),
        name=f"dilated_branch_d{dil}",
    )(view, view, view, view, view, view, view, bias_b)
    return o.reshape(s, A_WIDTH), lse.reshape(s, ngroups * LANES)


def _dilated_merge_kernel(o1_ref, o2_ref, o3_ref, l1_ref, l2_ref, l3_ref, fb_ref, out_ref):
    g = pl.program_id(1)
    ngroups = A_HEADS // DIL_HG
    lanes_per_head = LANES // DIL_HG

    @pl.when(g < ngroups)
    def _():
        ls = [l1_ref[...], l2_ref[...], l3_ref[...]]
        mx = jnp.maximum(jnp.maximum(ls[0], ls[1]), ls[2])
        es = [jnp.exp(l - mx) for l in ls]
        inv = 1.0 / (es[0] + es[1] + es[2])
        ws = [e * inv for e in es]
        for j in range(DIL_HG):
            hs = slice(HEAD_DIM * j, HEAD_DIM * (j + 1))
            lc = slice(lanes_per_head * j, lanes_per_head * j + 1)
            acc = ws[0][:, lc] * o1_ref[:, hs].astype(F32)
            acc += ws[1][:, lc] * o2_ref[:, hs].astype(F32)
            acc += ws[2][:, lc] * o3_ref[:, hs].astype(F32)
            out_ref[:, hs] = acc.astype(out_ref.dtype)

    @pl.when(g == ngroups)
    def _():
        out_ref[...] = fb_ref[...]


def _dilated_merge(os_, lses, fb, *, tm=512):
    s = fb.shape[0]
    ngroups = A_HEADS // DIL_HG
    gw = DIL_HG * HEAD_DIM
    assert fb.shape[1] == gw
    clamp = lambda i, g: (i, jnp.minimum(g, ngroups - 1))
    return pl.pallas_call(
        _dilated_merge_kernel,
        grid=(s // tm, ngroups + 1),
        in_specs=[pl.BlockSpec((tm, gw), clamp)] * 3 + [pl.BlockSpec((tm, LANES), clamp)] * 3
                 + [pl.BlockSpec((tm, gw), lambda i, g: (i, 0))],
        out_specs=pl.BlockSpec((tm, gw), lambda i, g: (i, g)),
        out_shape=jax.ShapeDtypeStruct((s, A_WIDTH + B_WIDTH), BF16),
        compiler_params=_params("parallel", "arbitrary"),
        name="dilated_merge",
    )(*os_, *lses, fb)


def _dft_cos_sin(n):
    idx = np.arange(n, dtype=np.int64)
    ang = 2.0 * np.pi * ((idx[:, None] * idx[None, :]) % n) / n
    return np.cos(ang), np.sin(ang)


def _fourier_weights_kernel(cs_ref, w_ref, o_ref, *, norm):
    w = w_ref[0]
    ab = jnp.dot(cs_ref[...], w, preferred_element_type=F32, precision=lax.Precision.HIGHEST) * norm
    o_ref[0] = jnp.concatenate([ab[:HEAD_DIM], ab[HEAD_DIM:]], axis=1).astype(o_ref.dtype)


def _fourier_weights(w_f, seq):
    c, s = _dft_cos_sin(HEAD_DIM)
    cs = jnp.asarray(np.concatenate([c, s], axis=0), F32)
    norm = 1.0 / math.sqrt(seq * HEAD_DIM)
    return pl.pallas_call(
        functools.partial(_fourier_weights_kernel, norm=norm),
        grid=(B_GROUPS,),
        in_specs=[pl.BlockSpec((2 * HEAD_DIM, HEAD_DIM), lambda g: (0, 0)),
                  pl.BlockSpec((1, HEAD_DIM, HEAD_DIM), lambda g: (g, 0, 0))],
        out_specs=pl.BlockSpec((1, HEAD_DIM, 2 * HEAD_DIM), lambda g: (g, 0, 0)),
        out_shape=jax.ShapeDtypeStruct((B_GROUPS, HEAD_DIM, 2 * HEAD_DIM), BF16),
        compiler_params=_params("arbitrary"),
        name="fourier_weights",
    )(cs, w_f)


def _fourier_channel_kernel(u_ref, ab_ref, y_ref, z_ref):
    for g in range(B_GROUPS):
        hs = slice(HEAD_DIM * g, HEAD_DIM * (g + 1))
        yz = _dot(u_ref[:, hs], ab_ref[g])
        y_ref[:, hs] = yz[:, :HEAD_DIM].astype(y_ref.dtype)
        z_ref[:, hs] = yz[:, HEAD_DIM:].astype(z_ref.dtype)


def _fourier_channel(proj, ab, *, tm=1024):
    s, w = proj.shape
    ublock = (w - B_WIDTH) // B_WIDTH
    assert ublock * B_WIDTH == w - B_WIDTH
    spec = pl.BlockSpec((tm, B_WIDTH), lambda i: (i, 0))
    return pl.pallas_call(
        _fourier_channel_kernel,
        grid=(s // tm,),
        in_specs=[pl.BlockSpec((tm, B_WIDTH), lambda i: (i, ublock)),
                  pl.BlockSpec((B_GROUPS, HEAD_DIM, 2 * HEAD_DIM), lambda i: (0, 0, 0))],
        out_specs=[spec, spec],
        out_shape=[jax.ShapeDtypeStruct((s, B_WIDTH), BF16)] * 2,
        compiler_params=_params("parallel"),
        name="fourier_channel",
    )(proj, ab)


def _fourier_stage1_kernel(m1_ref, tc_ref, ts_ref, y_ref, z_ref, tre_ref, tim_ref, *, n1, n2_per_step):
    yz = jnp.concatenate([y_ref[...], z_ref[...]], axis=0)
    ab = _dot(m1_ref[...], yz)
    a, b = ab[:n1], ab[n1:]
    ch = B_WIDTH
    for q in range(n2_per_step):
        cs = slice(ch * q, ch * (q + 1))
        c = tc_ref[0, :, q:q + 1]
        s = ts_ref[0, :, q:q + 1]
        aq, bq = a[:, cs], b[:, cs]
        tre_ref[:, cs] = (aq * c + bq * s).astype(tre_ref.dtype)
        tim_ref[:, cs] = (bq * c - aq * s).astype(tim_ref.dtype)


def _fourier_stage2_kernel(m2_ref, tre_ref, tim_ref, o_ref, *, n2, k1_per_step):
    ch = B_WIDTH
    for q in range(k1_per_step):
        rs = slice(n2 * q, n2 * (q + 1))
        t = jnp.concatenate([tre_ref[rs, :], tim_ref[rs, :]], axis=0)
        o_ref[:, ch * q:ch * (q + 1)] = _dot(m2_ref[...], t).astype(o_ref.dtype)


def _fourier_position(y, z, *, n2=FFT_N2, n2_per_step=16, k1_per_step=4):
    seq, ch = y.shape
    n1 = seq // n2
    n2_per_step = min(n2_per_step, n2)
    k1_per_step = min(k1_per_step, n1)
    c1, s1 = _dft_cos_sin(n1)
    m1 = jnp.asarray(np.block([[c1, -s1], [-s1, -c1]]), BF16)
    c2, s2 = _dft_cos_sin(n2)
    m2 = jnp.asarray(np.concatenate([c2, s2], axis=1), BF16)
    k1 = np.arange(n1, dtype=np.int64)[:, None]
    nn2 = np.arange(n2, dtype=np.int64)[None, :]
    ang = 2.0 * np.pi * ((k1 * nn2) % seq) / seq
    steps = n2 // n2_per_step
    tc = jnp.asarray(np.cos(ang).reshape(n1, steps, n2_per_step).transpose(1, 0, 2), F32)
    ts = jnp.asarray(np.sin(ang).reshape(n1, steps, n2_per_step).transpose(1, 0, 2), F32)

    cols = n2_per_step * ch
    dspec = pl.BlockSpec((n1, cols), lambda t: (0, t))
    tspec = pl.BlockSpec((1, n1, n2_per_step), lambda t: (t, 0, 0))
    tre, tim = pl.pallas_call(
        functools.partial(_fourier_stage1_kernel, n1=n1, n2_per_step=n2_per_step),
        grid=(steps,),
        in_specs=[pl.BlockSpec((2 * n1, 2 * n1), lambda t: (0, 0)), tspec, tspec, dspec, dspec],
        out_specs=[dspec, dspec],
        out_shape=[jax.ShapeDtypeStruct((n1, n2 * ch), BF16)] * 2,
        compiler_params=_params("parallel"),
        name="fourier_stage1",
    )(m1, tc, ts, y.reshape(n1, n2 * ch), z.reshape(n1, n2 * ch))

    tblock = pl.BlockSpec((k1_per_step * n2, ch), lambda t: (t, 0))
    out = pl.pallas_call(
        functools.partial(_fourier_stage2_kernel, n2=n2, k1_per_step=k1_per_step),
        grid=(n1 // k1_per_step,),
        in_specs=[pl.BlockSpec((n2, 2 * n2), lambda t: (0, 0)), tblock, tblock],
        out_specs=pl.BlockSpec((n2, k1_per_step * ch), lambda t: (0, t)),
        out_shape=jax.ShapeDtypeStruct((n2, n1 * ch), BF16),
        compiler_params=_params("parallel"),
        name="fourier_stage2",
    )(m2, tre.reshape(seq, ch), tim.reshape(seq, ch))
    return out.reshape(seq, ch)


def _diff_bias_kernel(tab_ref, o_ref, *, t):
    h = pl.program_id(0)
    row = lax.broadcasted_iota(jnp.int32, (t, t), 0)
    col = lax.broadcasted_iota(jnp.int32, (t, t), 1)
    base = col - row
    for r in range(2 * DIFF_R + 1):
        d = (r - DIFF_R) * t
        o_ref[0, r] = _bias_chain(base + d, d - (t - 1), d + (t - 1), lambda b: tab_ref[b, A_HEADS + h])


def _diff_bias(table, t):
    assert DIFF_R * t - (t - 1) >= _THR[_HALF_BUCKETS - 1]
    nt = 2 * DIFF_R + 1
    return pl.pallas_call(
        functools.partial(_diff_bias_kernel, t=t),
        grid=(C_HEADS,),
        in_specs=[pl.BlockSpec(memory_space=pltpu.SMEM)],
        out_specs=pl.BlockSpec((1, nt, t, t), lambda h: (h, 0, 0, 0)),
        out_shape=jax.ShapeDtypeStruct((C_HEADS, nt, t, t), F32),
        compiler_params=_params("arbitrary"),
        name="diff_bias",
    )(table)


def _diff_attn_kernel(q_ref, k_ref, v_ref, b_ref, lam_ref, g_ref, o_ref,
                      s_sc, p_sc, alpha_sc, m_sc, l_sc, acc_sc, *, t, lambda_init):
    qi = pl.program_id(1)
    nkv = k_ref.shape[0] // t
    nchunk = t // LANES

    def rows(j):
        return pl.ds(pl.multiple_of(j * t, t), t)

    def logits(j, buf):
        for c in range(2):
            hs = slice(HEAD_DIM * c, HEAD_DIM * (c + 1))
            s_sc[buf, c] = _dot_nt(q_ref[:, hs], k_ref[rows(j), hs])

    def softmax(j, buf, first=False):
        bias = b_ref[0, jnp.clip(j - qi, -DIFF_R, DIFF_R) + DIFF_R]
        for c in range(2):
            s = s_sc[buf, c] + bias
            chunks = [s[:, LANES * i:LANES * (i + 1)] for i in range(nchunk)]
            m_cur = jnp.max(functools.reduce(jnp.maximum, chunks), axis=-1, keepdims=True)
            if first:
                m_next = jnp.broadcast_to(m_cur, (t, LANES))
            else:
                m_prev = m_sc[c]
                m_next = jnp.maximum(m_prev, m_cur)
                alpha = jnp.exp(m_prev - m_next)
                alpha_sc[buf, c] = alpha
            ps = [jnp.exp(ch - m_next) for ch in chunks]
            row_sum = jnp.sum(functools.reduce(jnp.add, ps), axis=-1, keepdims=True)
            l_sc[c] = jnp.broadcast_to(row_sum, (t, LANES)) if first else alpha * l_sc[c] + row_sum
            m_sc[c] = m_next
            p_sc[buf, c] = jnp.concatenate(ps, axis=1).astype(BF16)

    def values(j, buf, first=False):
        for c in range(2):
            pv = _dot(p_sc[buf, c], v_ref[rows(j), :])
            if first:
                acc_sc[c] = pv
            else:
                alpha = alpha_sc[buf, c]
                acc_sc[c] = acc_sc[c] * jnp.concatenate([alpha, alpha], axis=1) + pv

    logits(0, 0)
    logits(1, 1)
    softmax(0, 0, first=True)

    def body(i, carry):
        n = 2 * i + 1
        logits(n + 1, 0)
        softmax(n, 1)
        values(n - 1, 0, first=False)
        logits(n + 2, 1)
        softmax(n + 1, 0)
        values(n, 1)
        return carry

    logits(2, 0)
    softmax(1, 1)
    values(0, 0, first=True)
    logits(3, 1)
    softmax(2, 0)
    values(1, 1)
    lax.fori_loop(1, nkv // 2 - 1, body, 0)
    softmax(nkv - 1, 1)
    values(nkv - 2, 0)
    values(nkv - 1, 1)

    lp = lam_ref[...]
    lam = (jnp.exp(jnp.sum(lp[0:1] * lp[1:2], axis=-1, keepdims=True))
           - jnp.exp(jnp.sum(lp[2:3] * lp[3:4], axis=-1, keepdims=True)) + lambda_init)
    inv0 = 1.0 / l_sc[0]
    inv1 = 1.0 / l_sc[1]
    o = (acc_sc[0] * jnp.concatenate([inv0, inv0], axis=1)
         - lam * (acc_sc[1] * jnp.concatenate([inv1, inv1], axis=1)))
    o_ref[...] = (_rms(o, g_ref[...]) * (1.0 - lambda_init)).astype(o_ref.dtype)


def _diff_attn(proj, bias, lam_params, subln_g, lambda_init, *, t):
    s = proj.shape[0]
    nt = 2 * DIFF_R + 1
    nkv = s // t
    assert nkv * t == s and nkv % 2 == 0 and nkv >= 4
    return pl.pallas_call(
        functools.partial(_diff_attn_kernel, t=t, lambda_init=lambda_init),
        grid=(C_HEADS, s // t),
        in_specs=[pl.BlockSpec((t, C_V_DIM), lambda h, i: (i, h)),
                  pl.BlockSpec((s, C_V_DIM), lambda h, i: (0, C_HEADS + h)),
                  pl.BlockSpec((s, C_V_DIM), lambda h, i: (0, 2 * C_HEADS + h)),
                  pl.BlockSpec((1, nt, t, t), lambda h, i: (h, 0, 0, 0)),
                  pl.BlockSpec((4, HEAD_DIM), lambda h, i: (0, 0)),
                  pl.BlockSpec((1, C_V_DIM), lambda h, i: (0, 0))],
        out_specs=pl.BlockSpec((t, C_V_DIM), lambda h, i: (i, h)),
        out_shape=jax.ShapeDtypeStruct((s, C_HEADS * C_V_DIM), BF16),
        scratch_shapes=[pltpu.VMEM((2, 2, t, t), F32), pltpu.VMEM((2, 2, t, t), BF16),
                        pltpu.VMEM((2, 2, t, LANES), F32),
                        pltpu.VMEM((2, t, LANES), F32), pltpu.VMEM((2, t, LANES), F32),
                        pltpu.VMEM((2, t, C_V_DIM), F32)],
        compiler_params=_params("parallel", "arbitrary"),
        name="diff_attn",
    )(proj, proj, proj, bias, lam_params, subln_g.reshape(1, C_V_DIM))


def kernel(x, norm_mix_g, norm_ffn_g, norm_final_g, rel_bias_table, w_in_even, w_fnet, w_out_even,
           w_qkv_odd, lambda_q1, lambda_k1, lambda_q2, lambda_k2, subln_g, w_out_odd, w_ff1, w_ff2):
    batch, seq, d_model = x.shape
    depth = norm_mix_g.shape[0]
    table = rel_bias_table.astype(F32)
    dil_bias = _dilated_bias(table)
    diff_t = min(DIFF_T, seq)
    diff_bias = _diff_bias(table, diff_t)
    c_qk_width = C_HEADS * 2 * HEAD_DIM
    odd_scale = jnp.concatenate([jnp.full((1, c_qk_width), 1.0 / math.sqrt(HEAD_DIM), F32),
                                 jnp.ones((1, w_qkv_odd.shape[2] - c_qk_width), F32)], axis=1)
    even_scale = jnp.ones((1, w_in_even.shape[2]), F32)

    outs = []
    for bidx in range(batch):
        xs = x[bidx]
        for i in range(depth):
            j = i // 2
            if i % 2 == 0:
                proj = _norm_matmul(xs, norm_mix_g[i], w_in_even[j].astype(BF16), even_scale)
                branches = [_dilated_branch(proj, dil_bias[bi], dil)
                            for bi, (_, dil) in enumerate(DILATED_CONFIGS)]
                y, z = _fourier_channel(proj, _fourier_weights(w_fnet[j], seq))
                fb = _fourier_position(y, z)
                mixed = _dilated_merge([b[0] for b in branches], [b[1] for b in branches], fb)
                xs = _matmul_residual(mixed, w_out_even[j].astype(BF16), xs)
            else:
                lambda_init = 0.8 - 0.6 * math.exp(-0.3 * i)
                proj = _norm_matmul(xs, norm_mix_g[i], w_qkv_odd[j].astype(BF16), odd_scale)
                lam_params = jnp.stack([lambda_q1[j], lambda_k1[j], lambda_q2[j], lambda_k2[j]]).astype(F32)
                attn = _diff_attn(proj, diff_bias, lam_params, subln_g[j], lambda_init, t=diff_t)
                xs = _matmul_residual(attn, w_out_odd[j].astype(BF16), xs)
            xs = _mlp(xs, norm_ffn_g[i], w_ff1[i].astype(BF16), w_ff2[i].astype(BF16), norm_final_g,
                      final_norm=(i == depth - 1))
        outs.append(xs)
    return jnp.stack(outs, axis=0)
```

```python
import functools
import math

import numpy as np
import jax
import jax.numpy as jnp
from jax import lax
from jax.experimental import pallas as pl
from jax.experimental.pallas import tpu as pltpu

F32 = jnp.float32
BF16 = jnp.bfloat16

HEAD_DIM = 128
A_HEADS = 12
B_GROUPS = 4
C_HEADS = 8
DILATED_CONFIGS = ((128, 1), (512, 4), (2048, 16))
NUM_BUCKETS = 32
MAX_DISTANCE = 1024
NORM_EPS = 1e-6
NEG_INF = -1e30

A_WIDTH = A_HEADS * HEAD_DIM
B_WIDTH = B_GROUPS * HEAD_DIM
C_V_DIM = 2 * HEAD_DIM
HALF_STEPS = 64
assert all(w // (2 * d) == HALF_STEPS for w, d in DILATED_CONFIGS)

LANES = 128
V7X_VMEM_LIMIT_BYTES = 56 * 1024 * 1024

DIL_TQ = 512
DIL_SUB = 128
DIL_BAND = DIL_SUB + 2 * HALF_STEPS
DIL_HG = 4
DIFF_T = 512
DIFF_R = 3
FFT_N2 = 128


def _params(*sem):
    return pltpu.CompilerParams(dimension_semantics=sem, vmem_limit_bytes=V7X_VMEM_LIMIT_BYTES)


def _magnitude_thresholds():
    half = NUM_BUCKETS // 2
    max_exact = half // 2
    n = np.arange(1, 4 * MAX_DISTANCE, dtype=np.int64)
    large = max_exact + (np.log(n / max_exact) / math.log(MAX_DISTANCE / max_exact)
                         * (half - max_exact)).astype(np.int64)
    large = np.minimum(large, half - 1)
    bucket = np.where(n < max_exact, n, large)
    assert np.all(np.diff(bucket) >= 0)
    thr = {b: int(n[np.argmax(bucket >= b)]) for b in range(1, half)}
    return thr


_THR = _magnitude_thresholds()
_HALF_BUCKETS = NUM_BUCKETS // 2


def _mag_bucket(n):
    return sum(1 for b in range(1, _HALF_BUCKETS) if n >= _THR[b])


def _bias_chain(rel, lo, hi, tab):
    n = jnp.abs(rel)

    def side(base, nlo, nhi):
        bmin, bmax = _mag_bucket(nlo), _mag_bucket(nhi)
        val = jnp.full(rel.shape, tab(base + bmax), F32)
        for b in range(bmax - 1, bmin - 1, -1):
            val = jnp.where(n < _THR[b + 1], tab(base + b), val)
        return val

    if lo > 0:
        return side(_HALF_BUCKETS, lo, hi)
    if hi <= 0:
        return side(0, -hi, -lo)
    return jnp.where(rel > 0, side(_HALF_BUCKETS, 1, hi), side(0, 0, -lo))


def _rms(xf, g):
    ms = jnp.mean(xf * xf, axis=-1, keepdims=True)
    return xf * lax.rsqrt(ms + NORM_EPS) * g


def _dot(a, b):
    return jnp.dot(a, b, preferred_element_type=F32)


def _dot_nt(a, b):
    return lax.dot_general(a, b, (((1,), (1,)), ((), ())), preferred_element_type=F32)


def _norm_matmul_kernel(x_ref, g_ref, w_ref, cs_ref, o_ref, *rest, class_dils):
    class_refs = rest[:len(class_dils)]
    hn_ref = rest[len(class_dils)]

    @pl.when(pl.program_id(1) == 0)
    def _():
        hn_ref[...] = _rms(x_ref[...], g_ref[...]).astype(BF16)

    y = _dot(hn_ref[...], w_ref[...]) * cs_ref[...]
    o_ref[...] = y.astype(o_ref.dtype)

    if class_dils:
        y_sc = rest[len(class_dils) + 1]
        tm, tn = y.shape
        for c in range(tn // LANES):
            y_sc[c] = y[:, LANES * c:LANES * (c + 1)]
        for ref, dil in zip(class_refs, class_dils):
            for r in range(dil):
                for c in range(tn // LANES):
                    ref[r, :, LANES * c:LANES * (c + 1)] = (
                        y_sc[c, pl.ds(r, tm // dil, stride=dil), :].astype(ref.dtype))


def _norm_matmul(x, g, w, col_scale, *, class_dils=(), tm=1024, tn=1024):
    m, d = x.shape
    n = w.shape[1]
    tm = min(tm, m)
    out_specs = [pl.BlockSpec((tm, tn), lambda i, j: (i, j))]
    out_shape = [jax.ShapeDtypeStruct((m, n), BF16)]
    scratch = [pltpu.VMEM((tm, d), BF16)]
    for dil in class_dils:
        out_specs.append(pl.BlockSpec((dil, tm // dil, tn), lambda i, j: (0, i, j)))
        out_shape.append(jax.ShapeDtypeStruct((dil, m // dil, n), BF16))
    if class_dils:
        scratch.append(pltpu.VMEM((tn // LANES, tm, LANES), F32))
    return pl.pallas_call(
        functools.partial(_norm_matmul_kernel, class_dils=tuple(class_dils)),
        grid=(m // tm, n // tn),
        in_specs=[pl.BlockSpec((tm, d), lambda i, j: (i, 0)),
                  pl.BlockSpec((1, d), lambda i, j: (0, 0)),
                  pl.BlockSpec((d, tn), lambda i, j: (0, j)),
                  pl.BlockSpec((1, tn), lambda i, j: (0, j))],
        out_specs=out_specs,
        out_shape=out_shape,
        scratch_shapes=scratch,
        compiler_params=_params("parallel", "arbitrary"),
        name="norm_matmul",
    )(x, g.reshape(1, d), w, col_scale)


def _matmul_residual_kernel(a_ref, w_ref, r_ref, o_ref):
    o_ref[...] = r_ref[...] + _dot(a_ref[...], w_ref[...])


def _matmul_residual(a, w, res, *, tm=512):
    m, k = a.shape
    n = w.shape[1]
    tm = min(tm, m)
    return pl.pallas_call(
        _matmul_residual_kernel,
        grid=(m // tm,),
        in_specs=[pl.BlockSpec((tm, k), lambda i: (i, 0)),
                  pl.BlockSpec((k, n), lambda i: (0, 0)),
                  pl.BlockSpec((tm, n), lambda i: (i, 0))],
        out_specs=pl.BlockSpec((tm, n), lambda i: (i, 0)),
        out_shape=jax.ShapeDtypeStruct((m, n), F32),
        compiler_params=_params("parallel"),
        name="matmul_residual",
    )(a, w, res)


def _mlp_kernel(x_ref, g_ref, w1_ref, w2_ref, gf_ref, o_ref, hn_ref, *, final_norm):
    c = pl.program_id(1)

    @pl.when(c == 0)
    def _():
        x = x_ref[...]
        hn_ref[...] = _rms(x, g_ref[...]).astype(BF16)
        o_ref[...] = x

    u = jnp.maximum(_dot(hn_ref[...], w1_ref[...]), 0.0)
    o_ref[...] += _dot((u * u).astype(BF16), w2_ref[...])

    if final_norm:
        @pl.when(c == pl.num_programs(1) - 1)
        def _():
            o_ref[...] = _rms(o_ref[...], gf_ref[...])


def _mlp(x, g, w1, w2, gf, *, final_norm, tm=1024, tf=512):
    m, d = x.shape
    f = w1.shape[1]
    tm = min(tm, m)
    return pl.pallas_call(
        functools.partial(_mlp_kernel, final_norm=final_norm),
        grid=(m // tm, f // tf),
        in_specs=[pl.BlockSpec((tm, d), lambda i, c: (i, 0)),
                  pl.BlockSpec((1, d), lambda i, c: (0, 0)),
                  pl.BlockSpec((d, tf), lambda i, c: (0, c)),
                  pl.BlockSpec((tf, d), lambda i, c: (c, 0)),
                  pl.BlockSpec((1, d), lambda i, c: (0, 0))],
        out_specs=pl.BlockSpec((tm, d), lambda i, c: (i, 0)),
        out_shape=jax.ShapeDtypeStruct((m, d), F32),
        scratch_shapes=[pltpu.VMEM((tm, d), BF16)],
        compiler_params=_params("parallel", "arbitrary"),
        name="mlp",
    )(x, g.reshape(1, d), w1, w2, gf.reshape(1, d))


def _dilated_bias_kernel(tab_ref, o_ref):
    h = pl.program_id(0)
    row = lax.broadcasted_iota(jnp.int32, (DIL_SUB, DIL_BAND), 0)
    col = lax.broadcasted_iota(jnp.int32, (DIL_SUB, DIL_BAND), 1)
    steps = col - row - HALF_STEPS
    in_band = jnp.abs(steps) <= HALF_STEPS
    for bi, (_, dil) in enumerate(DILATED_CONFIGS):
        reach = HALF_STEPS * dil
        rel = jnp.clip(steps, -HALF_STEPS, HALF_STEPS) * dil
        bias = _bias_chain(rel, -reach, reach, lambda b: tab_ref[b, h])
        o_ref[bi, 0] = jnp.where(in_band, bias, NEG_INF)


def _dilated_bias(table):
    nb = len(DILATED_CONFIGS)
    return pl.pallas_call(
        _dilated_bias_kernel,
        grid=(A_HEADS,),
        in_specs=[pl.BlockSpec(memory_space=pltpu.SMEM)],
        out_specs=pl.BlockSpec((nb, 1, DIL_SUB, DIL_BAND), lambda h: (0, h, 0, 0)),
        out_shape=jax.ShapeDtypeStruct((nb, A_HEADS, DIL_SUB, DIL_BAND), F32),
        compiler_params=_params("arbitrary"),
        name="dilated_bias",
    )(table)


def _dilated_branch_kernel(q_ref, kp_ref, kc_ref, kn_ref, vp_ref, vc_ref, vn_ref, b_ref,
                           o_ref, lse_ref, *, tq):
    tb = pl.program_id(2)
    last_tb = pl.num_programs(2) - 1
    nsub = tq // DIL_SUB
    scale = 1.0 / math.sqrt(HEAD_DIM)
    col = lax.broadcasted_iota(jnp.int32, (DIL_SUB, DIL_BAND), 1)
    lane = lax.broadcasted_iota(jnp.int32, (DIL_SUB, LANES), 1)
    lanes_per_head = LANES // DIL_HG

    def band(prev_ref, cur_ref, next_ref, i, hs):
        lo = DIL_SUB * i - HALF_STEPS
        hi = lo + DIL_BAND
        parts = []
        if lo < 0:
            parts.append(prev_ref[:, hs])
        parts.append(cur_ref[max(lo, 0):min(hi, tq), hs])
        if hi > tq:
            parts.append(next_ref[:, hs])
        return parts[0] if len(parts) == 1 else jnp.concatenate(parts, axis=0)

    for i in range(nsub):
        lse_blk = None
        for j in range(DIL_HG):
            hs = slice(HEAD_DIM * j, HEAD_DIM * (j + 1))
            q = q_ref[DIL_SUB * i:DIL_SUB * (i + 1), hs]
            kb = band(kp_ref, kc_ref, kn_ref, i, hs)
            vb = band(vp_ref, vc_ref, vn_ref, i, hs)
            s = _dot_nt(q, kb) * scale + b_ref[j]
            if i == 0:
                s = jnp.where((tb > 0) | (col >= HALF_STEPS), s, NEG_INF)
            if i == nsub - 1:
                s = jnp.where((tb < last_tb) | (col < DIL_BAND - HALF_STEPS), s, NEG_INF)
            m = jnp.max(s, axis=-1, keepdims=True)
            p = jnp.exp(s - m)
            den = jnp.sum(p, axis=-1, keepdims=True)
            o = _dot(p.astype(BF16), vb) / den
            o_ref[DIL_SUB * i:DIL_SUB * (i + 1), hs] = o.astype(o_ref.dtype)
            lse = m + jnp.log(den)
            lse_blk = lse if lse_blk is None else jnp.where(lane >= lanes_per_head * j, lse, lse_blk)
        lse_ref[DIL_SUB * i:DIL_SUB * (i + 1), :] = jnp.broadcast_to(lse_blk, (DIL_SUB, LANES))


def _dilated_branch(proj_c, bias_b, *, tq=DIL_TQ):
    dil, l, w = proj_c.shape
    tq = min(tq, l)
    ngroups = A_HEADS // DIL_HG
    gw = DIL_HG * HEAD_DIM
    qoff, koff, voff = 0, A_WIDTH // gw, 2 * A_WIDTH // gw
    halo_per_tq = tq // HALF_STEPS
    n_halo = l // HALF_STEPS

    def cur(off):
        return pl.BlockSpec((None, tq, gw), lambda r, g, t: (r, t, off + g))

    def prev(off):
        return pl.BlockSpec((None, HALF_STEPS, gw),
                            lambda r, g, t: (r, jnp.maximum(t * halo_per_tq - 1, 0), off + g))

    def nxt(off):
        return pl.BlockSpec((None, HALF_STEPS, gw),
                            lambda r, g, t: (r, jnp.minimum((t + 1) * halo_per_tq, n_halo - 1), off + g))

    return pl.pallas_call(
        functools.partial(_dilated_branch_kernel, tq=tq),
        grid=(dil, ngroups, l // tq),
        in_specs=[cur(qoff), prev(koff), cur(koff), nxt(koff), prev(voff), cur(voff), nxt(voff),
                  pl.BlockSpec((DIL_HG, DIL_SUB, DIL_BAND), lambda r, g, t: (g, 0, 0))],
        out_specs=[pl.BlockSpec((None, tq, gw), lambda r, g, t: (r, t, g)),
                   pl.BlockSpec((None, tq, LANES), lambda r, g, t: (r, t, g))],
        out_shape=[jax.ShapeDtypeStruct((dil, l, A_WIDTH), BF16),
                   jax.ShapeDtypeStruct((dil, l, ngroups * LANES), F32)],
        compiler_params=_params("parallel", "parallel", "arbitrary"),
        name=f"dilated_branch_d{dil}",
    )(*([proj_c] * 7), bias_b)


def _dilated_merge_kernel(*refs, dils):
    nb = len(dils)
    o_refs, l_refs = refs[:nb], refs[nb:2 * nb]
    fb_ref, out_ref, l_sc, o_sc = refs[2 * nb:]
    g = pl.program_id(1)
    ngroups = A_HEADS // DIL_HG
    lanes_per_head = LANES // DIL_HG
    tm = out_ref.shape[0]

    @pl.when(g < ngroups)
    def _():
        for b, dil in enumerate(dils):
            for r in range(dil):
                rows = pl.ds(r, tm // dil, stride=dil)
                l_sc[b, rows, :] = l_refs[b][r]
                for j in range(DIL_HG):
                    o_sc[b, j, rows, :] = o_refs[b][r, :, HEAD_DIM * j:HEAD_DIM * (j + 1)].astype(F32)
        ls = [l_sc[b] for b in range(nb)]
        mx = functools.reduce(jnp.maximum, ls)
        es = [jnp.exp(l - mx) for l in ls]
        inv = 1.0 / functools.reduce(jnp.add, es)
        for j in range(DIL_HG):
            lc = slice(lanes_per_head * j, lanes_per_head * j + 1)
            acc = functools.reduce(jnp.add, [(es[b] * inv)[:, lc] * o_sc[b, j] for b in range(nb)])
            out_ref[:, HEAD_DIM * j:HEAD_DIM * (j + 1)] = acc.astype(out_ref.dtype)

    @pl.when(g == ngroups)
    def _():
        out_ref[...] = fb_ref[...]


def _dilated_merge(os_, lses, fb, *, tm=512):
    s = fb.shape[0]
    tm = min(tm, s)
    ngroups = A_HEADS // DIL_HG
    gw = DIL_HG * HEAD_DIM
    assert fb.shape[1] == gw and HEAD_DIM == LANES
    dils = tuple(o.shape[0] for o in os_)
    clamp = lambda i, g: (0, i, jnp.minimum(g, ngroups - 1))
    return pl.pallas_call(
        functools.partial(_dilated_merge_kernel, dils=dils),
        grid=(s // tm, ngroups + 1),
        in_specs=[pl.BlockSpec((d, tm // d, gw), clamp) for d in dils]
                 + [pl.BlockSpec((d, tm // d, LANES), clamp) for d in dils]
                 + [pl.BlockSpec((tm, gw), lambda i, g: (i, 0))],
        out_specs=pl.BlockSpec((tm, gw), lambda i, g: (i, g)),
        out_shape=jax.ShapeDtypeStruct((s, A_WIDTH + B_WIDTH), BF16),
        scratch_shapes=[pltpu.VMEM((len(dils), tm, LANES), F32),
                        pltpu.VMEM((len(dils), DIL_HG, tm, LANES), F32)],
        compiler_params=_params("parallel", "arbitrary"),
        name="dilated_merge",
    )(*os_, *lses, fb)


def _dft_cos_sin(n):
    idx = np.arange(n, dtype=np.int64)
    ang = 2.0 * np.pi * ((idx[:, None] * idx[None, :]) % n) / n
    return np.cos(ang), np.sin(ang)


def _fourier_weights_kernel(cs_ref, w_ref, o_ref, *, norm):
    w = w_ref[0]
    ab = jnp.dot(cs_ref[...], w, preferred_element_type=F32, precision=lax.Precision.HIGHEST) * norm
    o_ref[0] = jnp.concatenate([ab[:HEAD_DIM], ab[HEAD_DIM:]], axis=1).astype(o_ref.dtype)


def _fourier_weights(w_f, seq):
    c, s = _dft_cos_sin(HEAD_DIM)
    cs = jnp.asarray(np.concatenate([c, s], axis=0), F32)
    norm = 1.0 / math.sqrt(seq * HEAD_DIM)
    return pl.pallas_call(
        functools.partial(_fourier_weights_kernel, norm=norm),
        grid=(B_GROUPS,),
        in_specs=[pl.BlockSpec((2 * HEAD_DIM, HEAD_DIM), lambda g: (0, 0)),
                  pl.BlockSpec((1, HEAD_DIM, HEAD_DIM), lambda g: (g, 0, 0))],
        out_specs=pl.BlockSpec((1, HEAD_DIM, 2 * HEAD_DIM), lambda g: (g, 0, 0)),
        out_shape=jax.ShapeDtypeStruct((B_GROUPS, HEAD_DIM, 2 * HEAD_DIM), BF16),
        compiler_params=_params("arbitrary"),
        name="fourier_weights",
    )(cs, w_f)


def _fourier_channel_kernel(u_ref, ab_ref, y_ref, z_ref):
    for g in range(B_GROUPS):
        hs = slice(HEAD_DIM * g, HEAD_DIM * (g + 1))
        yz = _dot(u_ref[:, hs], ab_ref[g])
        y_ref[:, hs] = yz[:, :HEAD_DIM].astype(y_ref.dtype)
        z_ref[:, hs] = yz[:, HEAD_DIM:].astype(z_ref.dtype)


def _fourier_channel(proj, ab, *, tm=1024):
    s, w = proj.shape
    ublock = (w - B_WIDTH) // B_WIDTH
    assert ublock * B_WIDTH == w - B_WIDTH
    spec = pl.BlockSpec((tm, B_WIDTH), lambda i: (i, 0))
    return pl.pallas_call(
        _fourier_channel_kernel,
        grid=(s // tm,),
        in_specs=[pl.BlockSpec((tm, B_WIDTH), lambda i: (i, ublock)),
                  pl.BlockSpec((B_GROUPS, HEAD_DIM, 2 * HEAD_DIM), lambda i: (0, 0, 0))],
        out_specs=[spec, spec],
        out_shape=[jax.ShapeDtypeStruct((s, B_WIDTH), BF16)] * 2,
        compiler_params=_params("parallel"),
        name="fourier_channel",
    )(proj, ab)


def _fourier_stage1_kernel(m1_ref, tc_ref, ts_ref, y_ref, z_ref, tre_ref, tim_ref, *, n1, n2_per_step):
    yz = jnp.concatenate([y_ref[...], z_ref[...]], axis=0)
    ab = _dot(m1_ref[...], yz)
    a, b = ab[:n1], ab[n1:]
    ch = B_WIDTH
    for q in range(n2_per_step):
        cs = slice(ch * q, ch * (q + 1))
        c = tc_ref[0, :, q:q + 1]
        s = ts_ref[0, :, q:q + 1]
        aq, bq = a[:, cs], b[:, cs]
        tre_ref[:, cs] = (aq * c + bq * s).astype(tre_ref.dtype)
        tim_ref[:, cs] = (bq * c - aq * s).astype(tim_ref.dtype)


def _fourier_stage2_kernel(m2_ref, tre_ref, tim_ref, o_ref, *, n2, k1_per_step):
    ch = B_WIDTH
    for q in range(k1_per_step):
        rs = slice(n2 * q, n2 * (q + 1))
        t = jnp.concatenate([tre_ref[rs, :], tim_ref[rs, :]], axis=0)
        o_ref[:, ch * q:ch * (q + 1)] = _dot(m2_ref[...], t).astype(o_ref.dtype)


def _fourier_position(y, z, *, n2=FFT_N2, n2_per_step=16, k1_per_step=4):
    seq, ch = y.shape
    n1 = seq // n2
    n2_per_step = min(n2_per_step, n2)
    k1_per_step = min(k1_per_step, n1)
    c1, s1 = _dft_cos_sin(n1)
    m1 = jnp.asarray(np.block([[c1, -s1], [-s1, -c1]]), BF16)
    c2, s2 = _dft_cos_sin(n2)
    m2 = jnp.asarray(np.concatenate([c2, s2], axis=1), BF16)
    k1 = np.arange(n1, dtype=np.int64)[:, None]
    nn2 = np.arange(n2, dtype=np.int64)[None, :]
    ang = 2.0 * np.pi * ((k1 * nn2) % seq) / seq
    steps = n2 // n2_per_step
    tc = jnp.asarray(np.cos(ang).reshape(n1, steps, n2_per_step).transpose(1, 0, 2), F32)
    ts = jnp.asarray(np.sin(ang).reshape(n1, steps, n2_per_step).transpose(1, 0, 2), F32)

    cols = n2_per_step * ch
    dspec = pl.BlockSpec((n1, cols), lambda t: (0, t))
    tspec = pl.BlockSpec((1, n1, n2_per_step), lambda t: (t, 0, 0))
    tre, tim = pl.pallas_call(
        functools.partial(_fourier_stage1_kernel, n1=n1, n2_per_step=n2_per_step),
        grid=(steps,),
        in_specs=[pl.BlockSpec((2 * n1, 2 * n1), lambda t: (0, 0)), tspec, tspec, dspec, dspec],
        out_specs=[dspec, dspec],
        out_shape=[jax.ShapeDtypeStruct((n1, n2 * ch), BF16)] * 2,
        compiler_params=_params("parallel"),
        name="fourier_stage1",
    )(m1, tc, ts, y.reshape(n1, n2 * ch), z.reshape(n1, n2 * ch))

    tblock = pl.BlockSpec((k1_per_step * n2, ch), lambda t: (t, 0))
    out = pl.pallas_call(
        functools.partial(_fourier_stage2_kernel, n2=n2, k1_per_step=k1_per_step),
        grid=(n1 // k1_per_step,),
        in_specs=[pl.BlockSpec((n2, 2 * n2), lambda t: (0, 0)), tblock, tblock],
        out_specs=pl.BlockSpec((n2, k1_per_step * ch), lambda t: (0, t)),
        out_shape=jax.ShapeDtypeStruct((n2, n1 * ch), BF16),
        compiler_params=_params("parallel"),
        name="fourier_stage2",
    )(m2, tre.reshape(seq, ch), tim.reshape(seq, ch))
    return out.reshape(seq, ch)


def _diff_bias_kernel(tab_ref, o_ref, *, t):
    h = pl.program_id(0)
    row = lax.broadcasted_iota(jnp.int32, (t, t), 0)
    col = lax.broadcasted_iota(jnp.int32, (t, t), 1)
    base = col - row
    for r in range(2 * DIFF_R + 1):
        d = (r - DIFF_R) * t
        o_ref[0, r] = _bias_chain(base + d, d - (t - 1), d + (t - 1), lambda b: tab_ref[b, A_HEADS + h])


def _diff_bias(table, t):
    assert DIFF_R * t - (t - 1) >= _THR[_HALF_BUCKETS - 1]
    nt = 2 * DIFF_R + 1
    return pl.pallas_call(
        functools.partial(_diff_bias_kernel, t=t),
        grid=(C_HEADS,),
        in_specs=[pl.BlockSpec(memory_space=pltpu.SMEM)],
        out_specs=pl.BlockSpec((1, nt, t, t), lambda h: (h, 0, 0, 0)),
        out_shape=jax.ShapeDtypeStruct((C_HEADS, nt, t, t), F32),
        compiler_params=_params("arbitrary"),
        name="diff_bias",
    )(table)


def _diff_attn_kernel(q_ref, k_ref, v_ref, b_ref, lam_ref, g_ref, o_ref,
                      s_sc, p_sc, alpha_sc, m_sc, l_sc, acc_sc, *, t, lambda_init):
    qi = pl.program_id(1)
    nkv = k_ref.shape[0] // t
    nchunk = t // LANES

    def rows(j):
        return pl.ds(pl.multiple_of(j * t, t), t)

    def logits(j, buf):
        for c in range(2):
            hs = slice(HEAD_DIM * c, HEAD_DIM * (c + 1))
            s_sc[buf, c] = _dot_nt(q_ref[:, hs], k_ref[rows(j), hs])

    def softmax(j, buf, first=False):
        bias = b_ref[0, jnp.clip(j - qi, -DIFF_R, DIFF_R) + DIFF_R]
        for c in range(2):
            s = s_sc[buf, c] + bias
            chunks = [s[:, LANES * i:LANES * (i + 1)] for i in range(nchunk)]
            m_cur = jnp.max(functools.reduce(jnp.maximum, chunks), axis=-1, keepdims=True)
            if first:
                m_next = jnp.broadcast_to(m_cur, (t, LANES))
            else:
                m_prev = m_sc[c]
                m_next = jnp.maximum(m_prev, m_cur)
                alpha = jnp.exp(m_prev - m_next)
                alpha_sc[buf, c] = alpha
            ps = [jnp.exp(ch - m_next) for ch in chunks]
            row_sum = jnp.sum(functools.reduce(jnp.add, ps), axis=-1, keepdims=True)
            l_sc[c] = jnp.broadcast_to(row_sum, (t, LANES)) if first else alpha * l_sc[c] + row_sum
            m_sc[c] = m_next
            p_sc[buf, c] = jnp.concatenate(ps, axis=1).astype(BF16)

    def values(j, buf, first=False):
        for c in range(2):
            pv = _dot(p_sc[buf, c], v_ref[rows(j), :])
            if first:
                acc_sc[c] = pv
            else:
                alpha = alpha_sc[buf, c]
                acc_sc[c] = acc_sc[c] * jnp.concatenate([alpha, alpha], axis=1) + pv

    logits(0, 0)
    logits(1, 1)
    softmax(0, 0, first=True)

    def body(i, carry):
        n = 2 * i + 1
        logits(n + 1, 0)
        softmax(n, 1)
        values(n - 1, 0, first=False)
        logits(n + 2, 1)
        softmax(n + 1, 0)
        values(n, 1)
        return carry

    logits(2, 0)
    softmax(1, 1)
    values(0, 0, first=True)
    logits(3, 1)
    softmax(2, 0)
    values(1, 1)
    lax.fori_loop(1, nkv // 2 - 1, body, 0)
    softmax(nkv - 1, 1)
    values(nkv - 2, 0)
    values(nkv - 1, 1)

    lp = lam_ref[...]
    lam = (jnp.exp(jnp.sum(lp[0:1] * lp[1:2], axis=-1, keepdims=True))
           - jnp.exp(jnp.sum(lp[2:3] * lp[3:4], axis=-1, keepdims=True)) + lambda_init)
    inv0 = 1.0 / l_sc[0]
    inv1 = 1.0 / l_sc[1]
    o = (acc_sc[0] * jnp.concatenate([inv0, inv0], axis=1)
         - lam * (acc_sc[1] * jnp.concatenate([inv1, inv1], axis=1)))
    o_ref[...] = (_rms(o, g_ref[...]) * (1.0 - lambda_init)).astype(o_ref.dtype)


def _diff_attn(proj, bias, lam_params, subln_g, lambda_init, *, t):
    s = proj.shape[0]
    nt = 2 * DIFF_R + 1
    nkv = s // t
    assert nkv * t == s and nkv % 2 == 0 and nkv >= 4
    return pl.pallas_call(
        functools.partial(_diff_attn_kernel, t=t, lambda_init=lambda_init),
        grid=(C_HEADS, s // t),
        in_specs=[pl.BlockSpec((t, C_V_DIM), lambda h, i: (i, h)),
                  pl.BlockSpec((s, C_V_DIM), lambda h, i: (0, C_HEADS + h)),
                  pl.BlockSpec((s, C_V_DIM), lambda h, i: (0, 2 * C_HEADS + h)),
                  pl.BlockSpec((1, nt, t, t), lambda h, i: (h, 0, 0, 0)),
                  pl.BlockSpec((4, HEAD_DIM), lambda h, i: (0, 0)),
                  pl.BlockSpec((1, C_V_DIM), lambda h, i: (0, 0))],
        out_specs=pl.BlockSpec((t, C_V_DIM), lambda h, i: (i, h)),
        out_shape=jax.ShapeDtypeStruct((s, C_HEADS * C_V_DIM), BF16),
        scratch_shapes=[pltpu.VMEM((2, 2, t, t), F32), pltpu.VMEM((2, 2, t, t), BF16),
                        pltpu.VMEM((2, 2, t, LANES), F32),
                        pltpu.VMEM((2, t, LANES), F32), pltpu.VMEM((2, t, LANES), F32),
                        pltpu.VMEM((2, t, C_V_DIM), F32)],
        compiler_params=_params("parallel", "arbitrary"),
        name="diff_attn",
    )(proj, proj, proj, bias, lam_params, subln_g.reshape(1, C_V_DIM))


def kernel(x, norm_mix_g, norm_ffn_g, norm_final_g, rel_bias_table, w_in_even, w_fnet, w_out_even,
           w_qkv_odd, lambda_q1, lambda_k1, lambda_q2, lambda_k2, subln_g, w_out_odd, w_ff1, w_ff2):
    batch, seq, d_model = x.shape
    depth = norm_mix_g.shape[0]
    table = rel_bias_table.astype(F32)
    dil_bias = _dilated_bias(table)
    diff_t = min(DIFF_T, seq)
    diff_bias = _diff_bias(table, diff_t)
    c_qk_width = C_HEADS * 2 * HEAD_DIM
    odd_scale = jnp.concatenate([jnp.full((1, c_qk_width), 1.0 / math.sqrt(HEAD_DIM), F32),
                                 jnp.ones((1, w_qkv_odd.shape[2] - c_qk_width), F32)], axis=1)
    even_scale = jnp.ones((1, w_in_even.shape[2]), F32)

    outs = []
    for bidx in range(batch):
        xs = x[bidx]
        for i in range(depth):
            j = i // 2
            if i % 2 == 0:
                class_dils = tuple(dil for _, dil in DILATED_CONFIGS if dil > 1)
                proj, *by_class = _norm_matmul(xs, norm_mix_g[i], w_in_even[j].astype(BF16), even_scale,
                                               class_dils=class_dils)
                by_class = dict(zip(class_dils, by_class))
                by_class[1] = proj.reshape(1, *proj.shape)
                branches = [_dilated_branch(by_class[dil], dil_bias[bi])
                            for bi, (_, dil) in enumerate(DILATED_CONFIGS)]
                y, z = _fourier_channel(proj, _fourier_weights(w_fnet[j], seq))
                fb = _fourier_position(y, z)
                mixed = _dilated_merge([b[0] for b in branches], [b[1] for b in branches], fb)
                xs = _matmul_residual(mixed, w_out_even[j].astype(BF16), xs)
            else:
                lambda_init = 0.8 - 0.6 * math.exp(-0.3 * i)
                proj, = _norm_matmul(xs, norm_mix_g[i], w_qkv_odd[j].astype(BF16), odd_scale)
                lam_params = jnp.stack([lambda_q1[j], lambda_k1[j], lambda_q2[j], lambda_k2[j]]).astype(F32)
                attn = _diff_attn(proj, diff_bias, lam_params, subln_g[j], lambda_init, t=diff_t)
                xs = _matmul_residual(attn, w_out_odd[j].astype(BF16), xs)
            xs = _mlp(xs, norm_ffn_g[i], w_ff1[i].astype(BF16), w_ff2[i].astype(BF16), norm_final_g,
                      final_norm=(i == depth - 1))
        outs.append(xs)
    return jnp.stack(outs, axis=0)
```

```python
import functools
import math

import numpy as np
import jax
import jax.numpy as jnp
from jax import lax
from jax.experimental import pallas as pl
from jax.experimental.pallas import tpu as pltpu

F32 = jnp.float32
BF16 = jnp.bfloat16

HEAD_DIM = 128
A_HEADS = 12
B_GROUPS = 4
C_HEADS = 8
DILATED_CONFIGS = ((128, 1), (512, 4), (2048, 16))
NUM_BUCKETS = 32
MAX_DISTANCE = 1024
NORM_EPS = 1e-6
NEG_INF = -1e30
LOG2_E = math.log2(math.e)

A_WIDTH = A_HEADS * HEAD_DIM
B_WIDTH = B_GROUPS * HEAD_DIM
C_V_DIM = 2 * HEAD_DIM
HALF_STEPS = 64
assert all(w // (2 * d) == HALF_STEPS for w, d in DILATED_CONFIGS)

LANES = 128
V7X_VMEM_LIMIT_BYTES = 56 * 1024 * 1024

DIL_TQ = 512
DIL_SUB = 128
DIL_BAND = DIL_SUB + 2 * HALF_STEPS
DIL_HG = 4
DIFF_T = 512
DIFF_R = 3
BF16_SUBLANES = 16
DIFF_VT_ROWS = C_V_DIM + BF16_SUBLANES
FFT_N2 = 128


def _params(*sem, flags=None):
    return pltpu.CompilerParams(dimension_semantics=sem, vmem_limit_bytes=V7X_VMEM_LIMIT_BYTES, flags=flags)


def _magnitude_thresholds():
    half = NUM_BUCKETS // 2
    max_exact = half // 2
    n = np.arange(1, 4 * MAX_DISTANCE, dtype=np.int64)
    large = max_exact + (np.log(n / max_exact) / math.log(MAX_DISTANCE / max_exact)
                         * (half - max_exact)).astype(np.int64)
    large = np.minimum(large, half - 1)
    bucket = np.where(n < max_exact, n, large)
    assert np.all(np.diff(bucket) >= 0)
    thr = {b: int(n[np.argmax(bucket >= b)]) for b in range(1, half)}
    return thr


_THR = _magnitude_thresholds()
_HALF_BUCKETS = NUM_BUCKETS // 2


def _mag_bucket(n):
    return sum(1 for b in range(1, _HALF_BUCKETS) if n >= _THR[b])


def _bias_chain(rel, lo, hi, tab):
    n = jnp.abs(rel)

    def side(base, nlo, nhi):
        bmin, bmax = _mag_bucket(nlo), _mag_bucket(nhi)
        val = jnp.full(rel.shape, tab(base + bmax), F32)
        for b in range(bmax - 1, bmin - 1, -1):
            val = jnp.where(n < _THR[b + 1], tab(base + b), val)
        return val

    if lo > 0:
        return side(_HALF_BUCKETS, lo, hi)
    if hi <= 0:
        return side(0, -hi, -lo)
    return jnp.where(rel > 0, side(_HALF_BUCKETS, 1, hi), side(0, 0, -lo))


def _rms(xf, g):
    ms = jnp.mean(xf * xf, axis=-1, keepdims=True)
    return xf * lax.rsqrt(ms + NORM_EPS) * g


def _dot(a, b):
    return jnp.dot(a, b, preferred_element_type=F32)


def _dot_nt(a, b):
    return lax.dot_general(a, b, (((1,), (1,)), ((), ())), preferred_element_type=F32)


def _norm_matmul_kernel(x_ref, g_ref, w_ref, cs_ref, o_ref, *rest, class_dils):
    class_refs = rest[:len(class_dils)]
    hn_ref = rest[len(class_dils)]

    @pl.when(pl.program_id(1) == 0)
    def _():
        hn_ref[...] = _rms(x_ref[...], g_ref[...]).astype(BF16)

    y = _dot(hn_ref[...], w_ref[...]) * cs_ref[...]
    o_ref[...] = y.astype(o_ref.dtype)

    if class_dils:
        y_sc = rest[len(class_dils) + 1]
        tm, tn = y.shape
        for c in range(tn // LANES):
            y_sc[c] = y[:, LANES * c:LANES * (c + 1)]
        for ref, dil in zip(class_refs, class_dils):
            for r in range(dil):
                for c in range(tn // LANES):
                    ref[r, :, LANES * c:LANES * (c + 1)] = (
                        y_sc[c, pl.ds(r, tm // dil, stride=dil), :].astype(ref.dtype))


def _norm_matmul(x, g, w, col_scale, *, class_dils=(), tm=1024, tn=1024):
    m, d = x.shape
    n = w.shape[1]
    tm = min(tm, m)
    out_specs = [pl.BlockSpec((tm, tn), lambda i, j: (i, j))]
    out_shape = [jax.ShapeDtypeStruct((m, n), BF16)]
    scratch = [pltpu.VMEM((tm, d), BF16)]
    for dil in class_dils:
        out_specs.append(pl.BlockSpec((dil, tm // dil, tn), lambda i, j: (0, i, j)))
        out_shape.append(jax.ShapeDtypeStruct((dil, m // dil, n), BF16))
    if class_dils:
        scratch.append(pltpu.VMEM((tn // LANES, tm, LANES), F32))
    return pl.pallas_call(
        functools.partial(_norm_matmul_kernel, class_dils=tuple(class_dils)),
        grid=(m // tm, n // tn),
        in_specs=[pl.BlockSpec((tm, d), lambda i, j: (i, 0)),
                  pl.BlockSpec((1, d), lambda i, j: (0, 0)),
                  pl.BlockSpec((d, tn), lambda i, j: (0, j)),
                  pl.BlockSpec((1, tn), lambda i, j: (0, j))],
        out_specs=out_specs,
        out_shape=out_shape,
        scratch_shapes=scratch,
        compiler_params=_params("parallel", "arbitrary"),
        name="norm_matmul",
    )(x, g.reshape(1, d), w, col_scale)


def _matmul_residual_kernel(a_ref, w_ref, r_ref, o_ref):
    o_ref[...] = r_ref[...] + _dot(a_ref[...], w_ref[...])


def _matmul_residual(a, w, res, *, tm=512):
    m, k = a.shape
    n = w.shape[1]
    tm = min(tm, m)
    return pl.pallas_call(
        _matmul_residual_kernel,
        grid=(m // tm,),
        in_specs=[pl.BlockSpec((tm, k), lambda i: (i, 0)),
                  pl.BlockSpec((k, n), lambda i: (0, 0)),
                  pl.BlockSpec((tm, n), lambda i: (i, 0))],
        out_specs=pl.BlockSpec((tm, n), lambda i: (i, 0)),
        out_shape=jax.ShapeDtypeStruct((m, n), F32),
        compiler_params=_params("parallel"),
        name="matmul_residual",
    )(a, w, res)


def _mlp_kernel(x_ref, g_ref, w1_ref, w2_ref, gf_ref, o_ref, hn_ref, *, final_norm):
    c = pl.program_id(1)

    @pl.when(c == 0)
    def _():
        x = x_ref[...]
        hn_ref[...] = _rms(x, g_ref[...]).astype(BF16)
        o_ref[...] = x

    u = jnp.maximum(_dot(hn_ref[...], w1_ref[...]), 0.0)
    o_ref[...] += _dot((u * u).astype(BF16), w2_ref[...])

    if final_norm:
        @pl.when(c == pl.num_programs(1) - 1)
        def _():
            o_ref[...] = _rms(o_ref[...], gf_ref[...])


def _mlp(x, g, w1, w2, gf, *, final_norm, tm=1024, tf=512):
    m, d = x.shape
    f = w1.shape[1]
    tm = min(tm, m)
    return pl.pallas_call(
        functools.partial(_mlp_kernel, final_norm=final_norm),
        grid=(m // tm, f // tf),
        in_specs=[pl.BlockSpec((tm, d), lambda i, c: (i, 0)),
                  pl.BlockSpec((1, d), lambda i, c: (0, 0)),
                  pl.BlockSpec((d, tf), lambda i, c: (0, c)),
                  pl.BlockSpec((tf, d), lambda i, c: (c, 0)),
                  pl.BlockSpec((1, d), lambda i, c: (0, 0))],
        out_specs=pl.BlockSpec((tm, d), lambda i, c: (i, 0)),
        out_shape=jax.ShapeDtypeStruct((m, d), F32),
        scratch_shapes=[pltpu.VMEM((tm, d), BF16)],
        compiler_params=_params("parallel", "arbitrary"),
        name="mlp",
    )(x, g.reshape(1, d), w1, w2, gf.reshape(1, d))


def _dilated_bias_kernel(tab_ref, o_ref):
    h = pl.program_id(0)
    row = lax.broadcasted_iota(jnp.int32, (DIL_SUB, DIL_BAND), 0)
    col = lax.broadcasted_iota(jnp.int32, (DIL_SUB, DIL_BAND), 1)
    steps = col - row - HALF_STEPS
    in_band = jnp.abs(steps) <= HALF_STEPS
    for bi, (_, dil) in enumerate(DILATED_CONFIGS):
        reach = HALF_STEPS * dil
        rel = jnp.clip(steps, -HALF_STEPS, HALF_STEPS) * dil
        bias = _bias_chain(rel, -reach, reach, lambda b: tab_ref[b, h] * LOG2_E)
        o_ref[bi, 0] = jnp.where(in_band, bias, NEG_INF)


def _dilated_bias(table):
    nb = len(DILATED_CONFIGS)
    return pl.pallas_call(
        _dilated_bias_kernel,
        grid=(A_HEADS,),
        in_specs=[pl.BlockSpec(memory_space=pltpu.SMEM)],
        out_specs=pl.BlockSpec((nb, 1, DIL_SUB, DIL_BAND), lambda h: (0, h, 0, 0)),
        out_shape=jax.ShapeDtypeStruct((nb, A_HEADS, DIL_SUB, DIL_BAND), F32),
        compiler_params=_params("arbitrary"),
        name="dilated_bias",
    )(table)


def _dilated_branch_kernel(q_ref, kp_ref, kc_ref, kn_ref, vp_ref, vc_ref, vn_ref, b_ref,
                           o_ref, lse_ref, *, tq):
    tb = pl.program_id(2)
    last_tb = pl.num_programs(2) - 1
    nsub = tq // DIL_SUB
    col = lax.broadcasted_iota(jnp.int32, (DIL_SUB, DIL_BAND), 1)
    lane = lax.broadcasted_iota(jnp.int32, (DIL_SUB, LANES), 1)
    lanes_per_head = LANES // DIL_HG

    def band(prev_ref, cur_ref, next_ref, i, hs):
        lo = DIL_SUB * i - HALF_STEPS
        hi = lo + DIL_BAND
        parts = []
        if lo < 0:
            parts.append(prev_ref[:, hs])
        parts.append(cur_ref[max(lo, 0):min(hi, tq), hs])
        if hi > tq:
            parts.append(next_ref[:, hs])
        return parts[0] if len(parts) == 1 else jnp.concatenate(parts, axis=0)

    for i in range(nsub):
        lse_blk = None
        for j in range(DIL_HG):
            hs = slice(HEAD_DIM * j, HEAD_DIM * (j + 1))
            q = q_ref[DIL_SUB * i:DIL_SUB * (i + 1), hs]
            kb = band(kp_ref, kc_ref, kn_ref, i, hs)
            vb = band(vp_ref, vc_ref, vn_ref, i, hs)
            s = _dot_nt(q, kb) + b_ref[j]
            if i == 0:
                s = jnp.where((tb > 0) | (col >= HALF_STEPS), s, NEG_INF)
            if i == nsub - 1:
                s = jnp.where((tb < last_tb) | (col < DIL_BAND - HALF_STEPS), s, NEG_INF)
            m = jnp.max(s, axis=-1, keepdims=True)
            p = jnp.exp2(s - m)
            den = jnp.sum(p, axis=-1, keepdims=True)
            o = _dot(p.astype(BF16), vb) / den
            o_ref[DIL_SUB * i:DIL_SUB * (i + 1), hs] = o.astype(o_ref.dtype)
            lse = m + jnp.log2(den)
            lse_blk = lse if lse_blk is None else jnp.where(lane >= lanes_per_head * j, lse, lse_blk)
        lse_ref[DIL_SUB * i:DIL_SUB * (i + 1), :] = jnp.broadcast_to(lse_blk, (DIL_SUB, LANES))


def _dilated_branch(proj_c, bias_b, *, tq=DIL_TQ):
    dil, l, w = proj_c.shape
    tq = min(tq, l)
    ngroups = A_HEADS // DIL_HG
    gw = DIL_HG * HEAD_DIM
    qoff, koff, voff = 0, A_WIDTH // gw, 2 * A_WIDTH // gw
    halo_per_tq = tq // HALF_STEPS
    n_halo = l // HALF_STEPS

    def cur(off):
        return pl.BlockSpec((None, tq, gw), lambda r, g, t: (r, t, off + g))

    def prev(off):
        return pl.BlockSpec((None, HALF_STEPS, gw),
                            lambda r, g, t: (r, jnp.maximum(t * halo_per_tq - 1, 0), off + g))

    def nxt(off):
        return pl.BlockSpec((None, HALF_STEPS, gw),
                            lambda r, g, t: (r, jnp.minimum((t + 1) * halo_per_tq, n_halo - 1), off + g))

    return pl.pallas_call(
        functools.partial(_dilated_branch_kernel, tq=tq),
        grid=(dil, ngroups, l // tq),
        in_specs=[cur(qoff), prev(koff), cur(koff), nxt(koff), prev(voff), cur(voff), nxt(voff),
                  pl.BlockSpec((DIL_HG, DIL_SUB, DIL_BAND), lambda r, g, t: (g, 0, 0))],
        out_specs=[pl.BlockSpec((None, tq, gw), lambda r, g, t: (r, t, g)),
                   pl.BlockSpec((None, tq, LANES), lambda r, g, t: (r, t, g))],
        out_shape=[jax.ShapeDtypeStruct((dil, l, A_WIDTH), BF16),
                   jax.ShapeDtypeStruct((dil, l, ngroups * LANES), F32)],
        compiler_params=_params("parallel", "parallel", "arbitrary"),
        name=f"dilated_branch_d{dil}",
    )(*([proj_c] * 7), bias_b)


def _dilated_merge_kernel(*refs, dils):
    nb = len(dils)
    o_refs, l_refs = refs[:nb], refs[nb:2 * nb]
    fb_ref, out_ref, l_sc, o_sc = refs[2 * nb:]
    g = pl.program_id(1)
    ngroups = A_HEADS // DIL_HG
    lanes_per_head = LANES // DIL_HG
    tm = out_ref.shape[0]

    @pl.when(g < ngroups)
    def _():
        for b, dil in enumerate(dils):
            for r in range(dil):
                rows = pl.ds(r, tm // dil, stride=dil)
                l_sc[b, rows, :] = l_refs[b][r]
                for j in range(DIL_HG):
                    o_sc[b, j, rows, :] = o_refs[b][r, :, HEAD_DIM * j:HEAD_DIM * (j + 1)].astype(F32)
        ls = [l_sc[b] for b in range(nb)]
        mx = functools.reduce(jnp.maximum, ls)
        es = [jnp.exp2(l - mx) for l in ls]
        inv = 1.0 / functools.reduce(jnp.add, es)
        for j in range(DIL_HG):
            lc = slice(lanes_per_head * j, lanes_per_head * j + 1)
            acc = functools.reduce(jnp.add, [(es[b] * inv)[:, lc] * o_sc[b, j] for b in range(nb)])
            out_ref[:, HEAD_DIM * j:HEAD_DIM * (j + 1)] = acc.astype(out_ref.dtype)

    @pl.when(g == ngroups)
    def _():
        out_ref[...] = fb_ref[...]


def _dilated_merge(os_, lses, fb, *, tm=512):
    s = fb.shape[0]
    tm = min(tm, s)
    ngroups = A_HEADS // DIL_HG
    gw = DIL_HG * HEAD_DIM
    assert fb.shape[1] == gw and HEAD_DIM == LANES
    dils = tuple(o.shape[0] for o in os_)
    clamp = lambda i, g: (0, i, jnp.minimum(g, ngroups - 1))
    return pl.pallas_call(
        functools.partial(_dilated_merge_kernel, dils=dils),
        grid=(s // tm, ngroups + 1),
        in_specs=[pl.BlockSpec((d, tm // d, gw), clamp) for d in dils]
                 + [pl.BlockSpec((d, tm // d, LANES), clamp) for d in dils]
                 + [pl.BlockSpec((tm, gw), lambda i, g: (i, 0))],
        out_specs=pl.BlockSpec((tm, gw), lambda i, g: (i, g)),
        out_shape=jax.ShapeDtypeStruct((s, A_WIDTH + B_WIDTH), BF16),
        scratch_shapes=[pltpu.VMEM((len(dils), tm, LANES), F32),
                        pltpu.VMEM((len(dils), DIL_HG, tm, LANES), F32)],
        compiler_params=_params("parallel", "arbitrary"),
        name="dilated_merge",
    )(*os_, *lses, fb)


def _dft_cos_sin(n):
    idx = np.arange(n, dtype=np.int64)
    ang = 2.0 * np.pi * ((idx[:, None] * idx[None, :]) % n) / n
    return np.cos(ang), np.sin(ang)


def _fourier_weights_kernel(cs_ref, w_ref, o_ref, *, norm):
    w = w_ref[0]
    ab = jnp.dot(cs_ref[...], w, preferred_element_type=F32, precision=lax.Precision.HIGHEST) * norm
    o_ref[0] = jnp.concatenate([ab[:HEAD_DIM], ab[HEAD_DIM:]], axis=1).astype(o_ref.dtype)


def _fourier_weights(w_f, seq):
    c, s = _dft_cos_sin(HEAD_DIM)
    cs = jnp.asarray(np.concatenate([c, s], axis=0), F32)
    norm = 1.0 / math.sqrt(seq * HEAD_DIM)
    return pl.pallas_call(
        functools.partial(_fourier_weights_kernel, norm=norm),
        grid=(B_GROUPS,),
        in_specs=[pl.BlockSpec((2 * HEAD_DIM, HEAD_DIM), lambda g: (0, 0)),
                  pl.BlockSpec((1, HEAD_DIM, HEAD_DIM), lambda g: (g, 0, 0))],
        out_specs=pl.BlockSpec((1, HEAD_DIM, 2 * HEAD_DIM), lambda g: (g, 0, 0)),
        out_shape=jax.ShapeDtypeStruct((B_GROUPS, HEAD_DIM, 2 * HEAD_DIM), BF16),
        compiler_params=_params("arbitrary"),
        name="fourier_weights",
    )(cs, w_f)


def _fourier_channel_kernel(u_ref, ab_ref, y_ref, z_ref):
    for g in range(B_GROUPS):
        hs = slice(HEAD_DIM * g, HEAD_DIM * (g + 1))
        yz = _dot(u_ref[:, hs], ab_ref[g])
        y_ref[:, hs] = yz[:, :HEAD_DIM].astype(y_ref.dtype)
        z_ref[:, hs] = yz[:, HEAD_DIM:].astype(z_ref.dtype)


def _fourier_channel(proj, ab, *, tm=1024):
    s, w = proj.shape
    ublock = (w - B_WIDTH) // B_WIDTH
    assert ublock * B_WIDTH == w - B_WIDTH
    spec = pl.BlockSpec((tm, B_WIDTH), lambda i: (i, 0))
    return pl.pallas_call(
        _fourier_channel_kernel,
        grid=(s // tm,),
        in_specs=[pl.BlockSpec((tm, B_WIDTH), lambda i: (i, ublock)),
                  pl.BlockSpec((B_GROUPS, HEAD_DIM, 2 * HEAD_DIM), lambda i: (0, 0, 0))],
        out_specs=[spec, spec],
        out_shape=[jax.ShapeDtypeStruct((s, B_WIDTH), BF16)] * 2,
        compiler_params=_params("parallel"),
        name="fourier_channel",
    )(proj, ab)


def _fourier_stage1_kernel(m1_ref, tc_ref, ts_ref, y_ref, z_ref, tre_ref, tim_ref, *, n1, n2_per_step):
    yz = jnp.concatenate([y_ref[...], z_ref[...]], axis=0)
    ab = _dot(m1_ref[...], yz)
    a, b = ab[:n1], ab[n1:]
    ch = B_WIDTH
    for q in range(n2_per_step):
        cs = slice(ch * q, ch * (q + 1))
        c = tc_ref[0, :, q:q + 1]
        s = ts_ref[0, :, q:q + 1]
        aq, bq = a[:, cs], b[:, cs]
        tre_ref[:, cs] = (aq * c + bq * s).astype(tre_ref.dtype)
        tim_ref[:, cs] = (bq * c - aq * s).astype(tim_ref.dtype)


def _fourier_stage2_kernel(m2_ref, tre_ref, tim_ref, o_ref, *, n2, k1_per_step):
    ch = B_WIDTH
    for q in range(k1_per_step):
        rs = slice(n2 * q, n2 * (q + 1))
        t = jnp.concatenate([tre_ref[rs, :], tim_ref[rs, :]], axis=0)
        o_ref[:, ch * q:ch * (q + 1)] = _dot(m2_ref[...], t).astype(o_ref.dtype)


def _fourier_position(y, z, *, n2=FFT_N2, n2_per_step=16, k1_per_step=4):
    seq, ch = y.shape
    n1 = seq // n2
    n2_per_step = min(n2_per_step, n2)
    k1_per_step = min(k1_per_step, n1)
    c1, s1 = _dft_cos_sin(n1)
    m1 = jnp.asarray(np.block([[c1, -s1], [-s1, -c1]]), BF16)
    c2, s2 = _dft_cos_sin(n2)
    m2 = jnp.asarray(np.concatenate([c2, s2], axis=1), BF16)
    k1 = np.arange(n1, dtype=np.int64)[:, None]
    nn2 = np.arange(n2, dtype=np.int64)[None, :]
    ang = 2.0 * np.pi * ((k1 * nn2) % seq) / seq
    steps = n2 // n2_per_step
    tc = jnp.asarray(np.cos(ang).reshape(n1, steps, n2_per_step).transpose(1, 0, 2), F32)
    ts = jnp.asarray(np.sin(ang).reshape(n1, steps, n2_per_step).transpose(1, 0, 2), F32)

    cols = n2_per_step * ch
    dspec = pl.BlockSpec((n1, cols), lambda t: (0, t))
    tspec = pl.BlockSpec((1, n1, n2_per_step), lambda t: (t, 0, 0))
    tre, tim = pl.pallas_call(
        functools.partial(_fourier_stage1_kernel, n1=n1, n2_per_step=n2_per_step),
        grid=(steps,),
        in_specs=[pl.BlockSpec((2 * n1, 2 * n1), lambda t: (0, 0)), tspec, tspec, dspec, dspec],
        out_specs=[dspec, dspec],
        out_shape=[jax.ShapeDtypeStruct((n1, n2 * ch), BF16)] * 2,
        compiler_params=_params("parallel"),
        name="fourier_stage1",
    )(m1, tc, ts, y.reshape(n1, n2 * ch), z.reshape(n1, n2 * ch))

    tblock = pl.BlockSpec((k1_per_step * n2, ch), lambda t: (t, 0))
    out = pl.pallas_call(
        functools.partial(_fourier_stage2_kernel, n2=n2, k1_per_step=k1_per_step),
        grid=(n1 // k1_per_step,),
        in_specs=[pl.BlockSpec((n2, 2 * n2), lambda t: (0, 0)), tblock, tblock],
        out_specs=pl.BlockSpec((n2, k1_per_step * ch), lambda t: (0, t)),
        out_shape=jax.ShapeDtypeStruct((n2, n1 * ch), BF16),
        compiler_params=_params("parallel"),
        name="fourier_stage2",
    )(m2, tre.reshape(seq, ch), tim.reshape(seq, ch))
    return out.reshape(seq, ch)


def _diff_bias_kernel(tab_ref, o_ref, *, t):
    h = pl.program_id(0)
    row = lax.broadcasted_iota(jnp.int32, (t, t), 0)
    col = lax.broadcasted_iota(jnp.int32, (t, t), 1)
    base = row - col
    for r in range(2 * DIFF_R + 1):
        d = (r - DIFF_R) * t
        o_ref[0, r] = _bias_chain(base + d, d - (t - 1), d + (t - 1),
                                  lambda b: tab_ref[b, A_HEADS + h] * LOG2_E)


def _diff_bias(table, t):
    assert DIFF_R * t - (t - 1) >= _THR[_HALF_BUCKETS - 1]
    nt = 2 * DIFF_R + 1
    return pl.pallas_call(
        functools.partial(_diff_bias_kernel, t=t),
        grid=(C_HEADS,),
        in_specs=[pl.BlockSpec(memory_space=pltpu.SMEM)],
        out_specs=pl.BlockSpec((1, nt, t, t), lambda h: (h, 0, 0, 0)),
        out_shape=jax.ShapeDtypeStruct((C_HEADS, nt, t, t), F32),
        compiler_params=_params("arbitrary"),
        name="diff_bias",
    )(table)


def _diff_attn_kernel(q_ref, k_ref, v_ref, b_ref, lam_ref, g_ref, o_ref,
                      vt_sc, s_sc, p_sc, alpha_sc, m_sc, acc_sc, *, t, lambda_init):
    qi = pl.program_id(1)
    nkv = k_ref.shape[0] // t

    @pl.when(qi == 0)
    def _():
        pad = lax.broadcasted_iota(jnp.int32, (DIFF_VT_ROWS - C_V_DIM, t), 0)
        ones_row = jnp.where(pad == 0, 1.0, 0.0).astype(BF16)
        for j in range(nkv):
            vt_sc[j, :C_V_DIM, :] = v_ref[t * j:t * (j + 1), :].astype(F32).T.astype(BF16)
            vt_sc[j, C_V_DIM:, :] = ones_row

    def rows(j):
        return pl.ds(pl.multiple_of(j * t, t), t)

    def logits(j, buf):
        bias = b_ref[0, jnp.clip(j - qi, -DIFF_R, DIFF_R) + DIFF_R]
        for c in range(2):
            hs = slice(HEAD_DIM * c, HEAD_DIM * (c + 1))
            s_sc[buf, c] = _dot_nt(k_ref[rows(j), hs], q_ref[:, hs]) + bias

    def softmax(j, buf, first=False):
        for c in range(2):
            m_cur = jnp.max(s_sc[buf, c], axis=0, keepdims=True)
            if first:
                m_next = m_cur
            else:
                m_prev = m_sc[c]
                m_next = jnp.maximum(m_prev, m_cur)
                alpha = jnp.exp2(m_prev - m_next)
                alpha_sc[buf, c] = alpha
            m_sc[c] = m_next
            p_sc[buf, c] = jnp.exp2(s_sc[buf, c] - m_next).astype(BF16)

    def values(j, buf, first=False):
        for c in range(2):
            pv = _dot(vt_sc[j], p_sc[buf, c])
            acc_sc[c] = pv if first else acc_sc[c] * alpha_sc[buf, c] + pv

    logits(0, 0)
    logits(1, 1)
    softmax(0, 0, first=True)

    def body(i, carry):
        n = 2 * i + 1
        logits(n + 1, 0)
        softmax(n, 1)
        values(n - 1, 0, first=False)
        logits(n + 2, 1)
        softmax(n + 1, 0)
        values(n, 1)
        return carry

    logits(2, 0)
    softmax(1, 1)
    values(0, 0, first=True)
    logits(3, 1)
    softmax(2, 0)
    values(1, 1)
    lax.fori_loop(1, nkv // 2 - 1, body, 0)
    softmax(nkv - 1, 1)
    values(nkv - 2, 0)
    values(nkv - 1, 1)

    lp = lam_ref[...]
    lam = (jnp.exp(jnp.sum(lp[0:1] * lp[1:2], axis=-1, keepdims=True))
           - jnp.exp(jnp.sum(lp[2:3] * lp[3:4], axis=-1, keepdims=True)) + lambda_init)
    num0, den0 = acc_sc[0, :C_V_DIM, :], acc_sc[0, C_V_DIM:C_V_DIM + 1, :]
    num1, den1 = acc_sc[1, :C_V_DIM, :], acc_sc[1, C_V_DIM:C_V_DIM + 1, :]
    o = num0 * (1.0 / den0) - lam * (num1 * (1.0 / den1))
    ms = jnp.mean(o * o, axis=0, keepdims=True)
    o = o * lax.rsqrt(ms + NORM_EPS) * (g_ref[...] * (1.0 - lambda_init))
    o_ref[...] = o.T.astype(o_ref.dtype)


def _diff_attn(proj, bias, lam_params, subln_g, lambda_init, *, t):
    s = proj.shape[0]
    nt = 2 * DIFF_R + 1
    nkv = s // t
    assert nkv * t == s and nkv % 2 == 0 and nkv >= 4
    return pl.pallas_call(
        functools.partial(_diff_attn_kernel, t=t, lambda_init=lambda_init),
        grid=(C_HEADS, s // t),
        in_specs=[pl.BlockSpec((t, C_V_DIM), lambda h, i: (i, h)),
                  pl.BlockSpec((s, C_V_DIM), lambda h, i: (0, C_HEADS + h)),
                  pl.BlockSpec((s, C_V_DIM), lambda h, i: (0, 2 * C_HEADS + h)),
                  pl.BlockSpec((1, nt, t, t), lambda h, i: (h, 0, 0, 0)),
                  pl.BlockSpec((4, HEAD_DIM), lambda h, i: (0, 0)),
                  pl.BlockSpec((C_V_DIM, 1), lambda h, i: (0, 0))],
        out_specs=pl.BlockSpec((t, C_V_DIM), lambda h, i: (i, h)),
        out_shape=jax.ShapeDtypeStruct((s, C_HEADS * C_V_DIM), BF16),
        scratch_shapes=[pltpu.VMEM((nkv, DIFF_VT_ROWS, t), BF16),
                        pltpu.VMEM((2, 2, t, t), F32), pltpu.VMEM((2, 2, t, t), BF16),
                        pltpu.VMEM((2, 2, 1, t), F32), pltpu.VMEM((2, 1, t), F32),
                        pltpu.VMEM((2, DIFF_VT_ROWS, t), F32)],
        compiler_params=_params("arbitrary", "arbitrary"),
        name="diff_attn",
    )(proj, proj, proj, bias, lam_params, subln_g.reshape(C_V_DIM, 1))


def kernel(x, norm_mix_g, norm_ffn_g, norm_final_g, rel_bias_table, w_in_even, w_fnet, w_out_even,
           w_qkv_odd, lambda_q1, lambda_k1, lambda_q2, lambda_k2, subln_g, w_out_odd, w_ff1, w_ff2):
    batch, seq, d_model = x.shape
    depth = norm_mix_g.shape[0]
    table = rel_bias_table.astype(F32)
    dil_bias = _dilated_bias(table)
    diff_t = min(DIFF_T, seq)
    diff_bias = _diff_bias(table, diff_t)
    c_qk_width = C_HEADS * 2 * HEAD_DIM
    odd_scale = jnp.concatenate([jnp.full((1, c_qk_width), LOG2_E / math.sqrt(HEAD_DIM), F32),
                                 jnp.ones((1, w_qkv_odd.shape[2] - c_qk_width), F32)], axis=1)
    even_scale = jnp.concatenate([jnp.full((1, A_WIDTH), LOG2_E / math.sqrt(HEAD_DIM), F32),
                                  jnp.ones((1, w_in_even.shape[2] - A_WIDTH), F32)], axis=1)

    outs = []
    for bidx in range(batch):
        xs = x[bidx]
        for i in range(depth):
            j = i // 2
            if i % 2 == 0:
                class_dils = tuple(dil for _, dil in DILATED_CONFIGS if dil > 1)
                proj, *by_class = _norm_matmul(xs, norm_mix_g[i], w_in_even[j].astype(BF16), even_scale,
                                               class_dils=class_dils)
                by_class = dict(zip(class_dils, by_class))
                by_class[1] = proj.reshape(1, *proj.shape)
                branches = [_dilated_branch(by_class[dil], dil_bias[bi])
                            for bi, (_, dil) in enumerate(DILATED_CONFIGS)]
                y, z = _fourier_channel(proj, _fourier_weights(w_fnet[j], seq))
                fb = _fourier_position(y, z)
                mixed = _dilated_merge([b[0] for b in branches], [b[1] for b in branches], fb)
                xs = _matmul_residual(mixed, w_out_even[j].astype(BF16), xs)
            else:
                lambda_init = 0.8 - 0.6 * math.exp(-0.3 * i)
                proj, = _norm_matmul(xs, norm_mix_g[i], w_qkv_odd[j].astype(BF16), odd_scale)
                lam_params = jnp.stack([lambda_q1[j], lambda_k1[j], lambda_q2[j], lambda_k2[j]]).astype(F32)
                attn = _diff_attn(proj, diff_bias, lam_params, subln_g[j], lambda_init, t=diff_t)
                xs = _matmul_residual(attn, w_out_odd[j].astype(BF16), xs)
            xs = _mlp(xs, norm_ffn_g[i], w_ff1[i].astype(BF16), w_ff2[i].astype(BF16), norm_final_g,
                      final_norm=(i == depth - 1))
        outs.append(xs)
    return jnp.stack(outs, axis=0)
```

```python
import functools
import math

import numpy as np
import jax
import jax.numpy as jnp
from jax import lax
from jax.experimental import pallas as pl
from jax.experimental.pallas import tpu as pltpu

F32 = jnp.float32
BF16 = jnp.bfloat16

HEAD_DIM = 128
A_HEADS = 12
B_GROUPS = 4
C_HEADS = 8
DILATED_CONFIGS = ((128, 1), (512, 4), (2048, 16))
NUM_BUCKETS = 32
MAX_DISTANCE = 1024
NORM_EPS = 1e-6
NEG_INF = -1e30
LOG2_E = math.log2(math.e)

A_WIDTH = A_HEADS * HEAD_DIM
B_WIDTH = B_GROUPS * HEAD_DIM
C_V_DIM = 2 * HEAD_DIM
HALF_STEPS = 64
assert all(w // (2 * d) == HALF_STEPS for w, d in DILATED_CONFIGS)

LANES = 128
V7X_VMEM_LIMIT_BYTES = 56 * 1024 * 1024

DIL_TQ = 512
DIL_SUB = 128
DIL_BAND = DIL_SUB + 2 * HALF_STEPS
DIL_HG = 4
DIFF_T = 512
DIFF_TK = 512
DIFF_TILES_PER_STEP = 4
DIFF_R = 3
BF16_SUBLANES = 16
DIFF_VT_ROWS = C_V_DIM + BF16_SUBLANES
FFT_N2 = 128


def _params(*sem, flags=None):
    return pltpu.CompilerParams(dimension_semantics=sem, vmem_limit_bytes=V7X_VMEM_LIMIT_BYTES, flags=flags)


def _magnitude_thresholds():
    half = NUM_BUCKETS // 2
    max_exact = half // 2
    n = np.arange(1, 4 * MAX_DISTANCE, dtype=np.int64)
    large = max_exact + (np.log(n / max_exact) / math.log(MAX_DISTANCE / max_exact)
                         * (half - max_exact)).astype(np.int64)
    large = np.minimum(large, half - 1)
    bucket = np.where(n < max_exact, n, large)
    assert np.all(np.diff(bucket) >= 0)
    thr = {b: int(n[np.argmax(bucket >= b)]) for b in range(1, half)}
    return thr


_THR = _magnitude_thresholds()
_HALF_BUCKETS = NUM_BUCKETS // 2


def _mag_bucket(n):
    return sum(1 for b in range(1, _HALF_BUCKETS) if n >= _THR[b])


def _bias_chain(rel, lo, hi, tab):
    n = jnp.abs(rel)

    def side(base, nlo, nhi):
        bmin, bmax = _mag_bucket(nlo), _mag_bucket(nhi)
        val = jnp.full(rel.shape, tab(base + bmax), F32)
        for b in range(bmax - 1, bmin - 1, -1):
            val = jnp.where(n < _THR[b + 1], tab(base + b), val)
        return val

    if lo > 0:
        return side(_HALF_BUCKETS, lo, hi)
    if hi <= 0:
        return side(0, -hi, -lo)
    return jnp.where(rel > 0, side(_HALF_BUCKETS, 1, hi), side(0, 0, -lo))


def _rms(xf, g):
    ms = jnp.mean(xf * xf, axis=-1, keepdims=True)
    return xf * lax.rsqrt(ms + NORM_EPS) * g


def _dot(a, b):
    return jnp.dot(a, b, preferred_element_type=F32)


def _dot_nt(a, b):
    return lax.dot_general(a, b, (((1,), (1,)), ((), ())), preferred_element_type=F32)


def _norm_matmul_kernel(x_ref, g_ref, w_ref, cs_ref, o_ref, *rest, class_dils):
    class_refs = rest[:len(class_dils)]
    hn_ref = rest[len(class_dils)]

    @pl.when(pl.program_id(1) == 0)
    def _():
        hn_ref[...] = _rms(x_ref[...], g_ref[...]).astype(BF16)

    y = _dot(hn_ref[...], w_ref[...]) * cs_ref[...]
    o_ref[...] = y.astype(o_ref.dtype)

    if class_dils:
        y_sc = rest[len(class_dils) + 1]
        tm, tn = y.shape
        for c in range(tn // LANES):
            y_sc[c] = y[:, LANES * c:LANES * (c + 1)]
        for ref, dil in zip(class_refs, class_dils):
            for r in range(dil):
                for c in range(tn // LANES):
                    ref[r, :, LANES * c:LANES * (c + 1)] = (
                        y_sc[c, pl.ds(r, tm // dil, stride=dil), :].astype(ref.dtype))


def _norm_matmul(x, g, w, col_scale, *, class_dils=(), tm=1024, tn=1024):
    m, d = x.shape
    n = w.shape[1]
    tm = min(tm, m)
    out_specs = [pl.BlockSpec((tm, tn), lambda i, j: (i, j))]
    out_shape = [jax.ShapeDtypeStruct((m, n), BF16)]
    scratch = [pltpu.VMEM((tm, d), BF16)]
    for dil in class_dils:
        out_specs.append(pl.BlockSpec((dil, tm // dil, tn), lambda i, j: (0, i, j)))
        out_shape.append(jax.ShapeDtypeStruct((dil, m // dil, n), BF16))
    if class_dils:
        scratch.append(pltpu.VMEM((tn // LANES, tm, LANES), F32))
    return pl.pallas_call(
        functools.partial(_norm_matmul_kernel, class_dils=tuple(class_dils)),
        grid=(m // tm, n // tn),
        in_specs=[pl.BlockSpec((tm, d), lambda i, j: (i, 0)),
                  pl.BlockSpec((1, d), lambda i, j: (0, 0)),
                  pl.BlockSpec((d, tn), lambda i, j: (0, j)),
                  pl.BlockSpec((1, tn), lambda i, j: (0, j))],
        out_specs=out_specs,
        out_shape=out_shape,
        scratch_shapes=scratch,
        compiler_params=_params("parallel", "arbitrary"),
        name="norm_matmul",
    )(x, g.reshape(1, d), w, col_scale)


def _matmul_residual_kernel(a_ref, w_ref, r_ref, o_ref):
    o_ref[...] = r_ref[...] + _dot(a_ref[...], w_ref[...])


def _matmul_residual(a, w, res, *, tm=512):
    m, k = a.shape
    n = w.shape[1]
    tm = min(tm, m)
    return pl.pallas_call(
        _matmul_residual_kernel,
        grid=(m // tm,),
        in_specs=[pl.BlockSpec((tm, k), lambda i: (i, 0)),
                  pl.BlockSpec((k, n), lambda i: (0, 0)),
                  pl.BlockSpec((tm, n), lambda i: (i, 0))],
        out_specs=pl.BlockSpec((tm, n), lambda i: (i, 0)),
        out_shape=jax.ShapeDtypeStruct((m, n), F32),
        compiler_params=_params("parallel"),
        name="matmul_residual",
    )(a, w, res)


def _mlp_kernel(x_ref, g_ref, w1_ref, w2_ref, gf_ref, o_ref, hn_ref, *, final_norm):
    c = pl.program_id(1)

    @pl.when(c == 0)
    def _():
        x = x_ref[...]
        hn_ref[...] = _rms(x, g_ref[...]).astype(BF16)
        o_ref[...] = x

    u = jnp.maximum(_dot(hn_ref[...], w1_ref[...]), 0.0)
    o_ref[...] += _dot((u * u).astype(BF16), w2_ref[...])

    if final_norm:
        @pl.when(c == pl.num_programs(1) - 1)
        def _():
            o_ref[...] = _rms(o_ref[...], gf_ref[...])


def _mlp(x, g, w1, w2, gf, *, final_norm, tm=1024, tf=512):
    m, d = x.shape
    f = w1.shape[1]
    tm = min(tm, m)
    return pl.pallas_call(
        functools.partial(_mlp_kernel, final_norm=final_norm),
        grid=(m // tm, f // tf),
        in_specs=[pl.BlockSpec((tm, d), lambda i, c: (i, 0)),
                  pl.BlockSpec((1, d), lambda i, c: (0, 0)),
                  pl.BlockSpec((d, tf), lambda i, c: (0, c)),
                  pl.BlockSpec((tf, d), lambda i, c: (c, 0)),
                  pl.BlockSpec((1, d), lambda i, c: (0, 0))],
        out_specs=pl.BlockSpec((tm, d), lambda i, c: (i, 0)),
        out_shape=jax.ShapeDtypeStruct((m, d), F32),
        scratch_shapes=[pltpu.VMEM((tm, d), BF16)],
        compiler_params=_params("parallel", "arbitrary"),
        name="mlp",
    )(x, g.reshape(1, d), w1, w2, gf.reshape(1, d))


def _dilated_bias_kernel(tab_ref, o_ref):
    h = pl.program_id(0)
    row = lax.broadcasted_iota(jnp.int32, (DIL_SUB, DIL_BAND), 0)
    col = lax.broadcasted_iota(jnp.int32, (DIL_SUB, DIL_BAND), 1)
    steps = col - row - HALF_STEPS
    in_band = jnp.abs(steps) <= HALF_STEPS
    for bi, (_, dil) in enumerate(DILATED_CONFIGS):
        reach = HALF_STEPS * dil
        rel = jnp.clip(steps, -HALF_STEPS, HALF_STEPS) * dil
        bias = _bias_chain(rel, -reach, reach, lambda b: tab_ref[b, h] * LOG2_E)
        o_ref[bi, 0] = jnp.where(in_band, bias, NEG_INF)


def _dilated_bias(table):
    nb = len(DILATED_CONFIGS)
    return pl.pallas_call(
        _dilated_bias_kernel,
        grid=(A_HEADS,),
        in_specs=[pl.BlockSpec(memory_space=pltpu.SMEM)],
        out_specs=pl.BlockSpec((nb, 1, DIL_SUB, DIL_BAND), lambda h: (0, h, 0, 0)),
        out_shape=jax.ShapeDtypeStruct((nb, A_HEADS, DIL_SUB, DIL_BAND), F32),
        compiler_params=_params("arbitrary"),
        name="dilated_bias",
    )(table)


def _dilated_branch_kernel(q_ref, kp_ref, kc_ref, kn_ref, vp_ref, vc_ref, vn_ref, b_ref,
                           o_ref, lse_ref, *, tq):
    tb = pl.program_id(2)
    last_tb = pl.num_programs(2) - 1
    nsub = tq // DIL_SUB
    col = lax.broadcasted_iota(jnp.int32, (DIL_SUB, DIL_BAND), 1)
    lane = lax.broadcasted_iota(jnp.int32, (DIL_SUB, LANES), 1)
    lanes_per_head = LANES // DIL_HG

    def band(prev_ref, cur_ref, next_ref, i, hs):
        lo = DIL_SUB * i - HALF_STEPS
        hi = lo + DIL_BAND
        parts = []
        if lo < 0:
            parts.append(prev_ref[:, hs])
        parts.append(cur_ref[max(lo, 0):min(hi, tq), hs])
        if hi > tq:
            parts.append(next_ref[:, hs])
        return parts[0] if len(parts) == 1 else jnp.concatenate(parts, axis=0)

    for i in range(nsub):
        lse_blk = None
        for j in range(DIL_HG):
            hs = slice(HEAD_DIM * j, HEAD_DIM * (j + 1))
            q = q_ref[DIL_SUB * i:DIL_SUB * (i + 1), hs]
            kb = band(kp_ref, kc_ref, kn_ref, i, hs)
            vb = band(vp_ref, vc_ref, vn_ref, i, hs)
            s = _dot_nt(q, kb) + b_ref[j]
            if i == 0:
                s = jnp.where((tb > 0) | (col >= HALF_STEPS), s, NEG_INF)
            if i == nsub - 1:
                s = jnp.where((tb < last_tb) | (col < DIL_BAND - HALF_STEPS), s, NEG_INF)
            m = jnp.max(s, axis=-1, keepdims=True)
            p = jnp.exp2(s - m)
            den = jnp.sum(p, axis=-1, keepdims=True)
            o = _dot(p.astype(BF16), vb) / den
            o_ref[DIL_SUB * i:DIL_SUB * (i + 1), hs] = o.astype(o_ref.dtype)
            lse = m + jnp.log2(den)
            lse_blk = lse if lse_blk is None else jnp.where(lane >= lanes_per_head * j, lse, lse_blk)
        lse_ref[DIL_SUB * i:DIL_SUB * (i + 1), :] = jnp.broadcast_to(lse_blk, (DIL_SUB, LANES))


def _dilated_branch(proj_c, bias_b, *, tq=DIL_TQ):
    dil, l, w = proj_c.shape
    tq = min(tq, l)
    ngroups = A_HEADS // DIL_HG
    gw = DIL_HG * HEAD_DIM
    qoff, koff, voff = 0, A_WIDTH // gw, 2 * A_WIDTH // gw
    halo_per_tq = tq // HALF_STEPS
    n_halo = l // HALF_STEPS

    def cur(off):
        return pl.BlockSpec((None, tq, gw), lambda r, g, t: (r, t, off + g))

    def prev(off):
        return pl.BlockSpec((None, HALF_STEPS, gw),
                            lambda r, g, t: (r, jnp.maximum(t * halo_per_tq - 1, 0), off + g))

    def nxt(off):
        return pl.BlockSpec((None, HALF_STEPS, gw),
                            lambda r, g, t: (r, jnp.minimum((t + 1) * halo_per_tq, n_halo - 1), off + g))

    return pl.pallas_call(
        functools.partial(_dilated_branch_kernel, tq=tq),
        grid=(dil, ngroups, l // tq),
        in_specs=[cur(qoff), prev(koff), cur(koff), nxt(koff), prev(voff), cur(voff), nxt(voff),
                  pl.BlockSpec((DIL_HG, DIL_SUB, DIL_BAND), lambda r, g, t: (g, 0, 0))],
        out_specs=[pl.BlockSpec((None, tq, gw), lambda r, g, t: (r, t, g)),
                   pl.BlockSpec((None, tq, LANES), lambda r, g, t: (r, t, g))],
        out_shape=[jax.ShapeDtypeStruct((dil, l, A_WIDTH), BF16),
                   jax.ShapeDtypeStruct((dil, l, ngroups * LANES), F32)],
        compiler_params=_params("parallel", "parallel", "arbitrary"),
        name=f"dilated_branch_d{dil}",
    )(*([proj_c] * 7), bias_b)


def _dilated_merge_kernel(*refs, dils):
    nb = len(dils)
    o_refs, l_refs = refs[:nb], refs[nb:2 * nb]
    fb_ref, out_ref, l_sc, o_sc = refs[2 * nb:]
    g = pl.program_id(1)
    ngroups = A_HEADS // DIL_HG
    lanes_per_head = LANES // DIL_HG
    tm = out_ref.shape[0]

    @pl.when(g < ngroups)
    def _():
        for b, dil in enumerate(dils):
            for r in range(dil):
                rows = pl.ds(r, tm // dil, stride=dil)
                l_sc[b, rows, :] = l_refs[b][r]
                for j in range(DIL_HG):
                    o_sc[b, j, rows, :] = o_refs[b][r, :, HEAD_DIM * j:HEAD_DIM * (j + 1)].astype(F32)
        ls = [l_sc[b] for b in range(nb)]
        mx = functools.reduce(jnp.maximum, ls)
        es = [jnp.exp2(l - mx) for l in ls]
        inv = 1.0 / functools.reduce(jnp.add, es)
        for j in range(DIL_HG):
            lc = slice(lanes_per_head * j, lanes_per_head * j + 1)
            acc = functools.reduce(jnp.add, [(es[b] * inv)[:, lc] * o_sc[b, j] for b in range(nb)])
            out_ref[:, HEAD_DIM * j:HEAD_DIM * (j + 1)] = acc.astype(out_ref.dtype)

    @pl.when(g == ngroups)
    def _():
        out_ref[...] = fb_ref[...]


def _dilated_merge(os_, lses, fb, *, tm=512):
    s = fb.shape[0]
    tm = min(tm, s)
    ngroups = A_HEADS // DIL_HG
    gw = DIL_HG * HEAD_DIM
    assert fb.shape[1] == gw and HEAD_DIM == LANES
    dils = tuple(o.shape[0] for o in os_)
    clamp = lambda i, g: (0, i, jnp.minimum(g, ngroups - 1))
    return pl.pallas_call(
        functools.partial(_dilated_merge_kernel, dils=dils),
        grid=(s // tm, ngroups + 1),
        in_specs=[pl.BlockSpec((d, tm // d, gw), clamp) for d in dils]
                 + [pl.BlockSpec((d, tm // d, LANES), clamp) for d in dils]
                 + [pl.BlockSpec((tm, gw), lambda i, g: (i, 0))],
        out_specs=pl.BlockSpec((tm, gw), lambda i, g: (i, g)),
        out_shape=jax.ShapeDtypeStruct((s, A_WIDTH + B_WIDTH), BF16),
        scratch_shapes=[pltpu.VMEM((len(dils), tm, LANES), F32),
                        pltpu.VMEM((len(dils), DIL_HG, tm, LANES), F32)],
        compiler_params=_params("parallel", "arbitrary"),
        name="dilated_merge",
    )(*os_, *lses, fb)


def _dft_cos_sin(n):
    idx = np.arange(n, dtype=np.int64)
    ang = 2.0 * np.pi * ((idx[:, None] * idx[None, :]) % n) / n
    return np.cos(ang), np.sin(ang)


def _fourier_weights_kernel(cs_ref, w_ref, o_ref, *, norm):
    w = w_ref[0]
    ab = jnp.dot(cs_ref[...], w, preferred_element_type=F32, precision=lax.Precision.HIGHEST) * norm
    o_ref[0] = jnp.concatenate([ab[:HEAD_DIM], ab[HEAD_DIM:]], axis=1).astype(o_ref.dtype)


def _fourier_weights(w_f, seq):
    c, s = _dft_cos_sin(HEAD_DIM)
    cs = jnp.asarray(np.concatenate([c, s], axis=0), F32)
    norm = 1.0 / math.sqrt(seq * HEAD_DIM)
    return pl.pallas_call(
        functools.partial(_fourier_weights_kernel, norm=norm),
        grid=(B_GROUPS,),
        in_specs=[pl.BlockSpec((2 * HEAD_DIM, HEAD_DIM), lambda g: (0, 0)),
                  pl.BlockSpec((1, HEAD_DIM, HEAD_DIM), lambda g: (g, 0, 0))],
        out_specs=pl.BlockSpec((1, HEAD_DIM, 2 * HEAD_DIM), lambda g: (g, 0, 0)),
        out_shape=jax.ShapeDtypeStruct((B_GROUPS, HEAD_DIM, 2 * HEAD_DIM), BF16),
        compiler_params=_params("arbitrary"),
        name="fourier_weights",
    )(cs, w_f)


def _fourier_channel_kernel(u_ref, ab_ref, y_ref, z_ref):
    for g in range(B_GROUPS):
        hs = slice(HEAD_DIM * g, HEAD_DIM * (g + 1))
        yz = _dot(u_ref[:, hs], ab_ref[g])
        y_ref[:, hs] = yz[:, :HEAD_DIM].astype(y_ref.dtype)
        z_ref[:, hs] = yz[:, HEAD_DIM:].astype(z_ref.dtype)


def _fourier_channel(proj, ab, *, tm=1024):
    s, w = proj.shape
    ublock = (w - B_WIDTH) // B_WIDTH
    assert ublock * B_WIDTH == w - B_WIDTH
    spec = pl.BlockSpec((tm, B_WIDTH), lambda i: (i, 0))
    return pl.pallas_call(
        _fourier_channel_kernel,
        grid=(s // tm,),
        in_specs=[pl.BlockSpec((tm, B_WIDTH), lambda i: (i, ublock)),
                  pl.BlockSpec((B_GROUPS, HEAD_DIM, 2 * HEAD_DIM), lambda i: (0, 0, 0))],
        out_specs=[spec, spec],
        out_shape=[jax.ShapeDtypeStruct((s, B_WIDTH), BF16)] * 2,
        compiler_params=_params("parallel"),
        name="fourier_channel",
    )(proj, ab)


def _fourier_stage1_kernel(m1_ref, tc_ref, ts_ref, y_ref, z_ref, tre_ref, tim_ref, *, n1, n2_per_step):
    yz = jnp.concatenate([y_ref[...], z_ref[...]], axis=0)
    ab = _dot(m1_ref[...], yz)
    a, b = ab[:n1], ab[n1:]
    ch = B_WIDTH
    for q in range(n2_per_step):
        cs = slice(ch * q, ch * (q + 1))
        c = tc_ref[0, :, q:q + 1]
        s = ts_ref[0, :, q:q + 1]
        aq, bq = a[:, cs], b[:, cs]
        tre_ref[:, cs] = (aq * c + bq * s).astype(tre_ref.dtype)
        tim_ref[:, cs] = (bq * c - aq * s).astype(tim_ref.dtype)


def _fourier_stage2_kernel(m2_ref, tre_ref, tim_ref, o_ref, *, n2, k1_per_step):
    ch = B_WIDTH
    for q in range(k1_per_step):
        rs = slice(n2 * q, n2 * (q + 1))
        t = jnp.concatenate([tre_ref[rs, :], tim_ref[rs, :]], axis=0)
        o_ref[:, ch * q:ch * (q + 1)] = _dot(m2_ref[...], t).astype(o_ref.dtype)


def _fourier_position(y, z, *, n2=FFT_N2, n2_per_step=16, k1_per_step=4):
    seq, ch = y.shape
    n1 = seq // n2
    n2_per_step = min(n2_per_step, n2)
    k1_per_step = min(k1_per_step, n1)
    c1, s1 = _dft_cos_sin(n1)
    m1 = jnp.asarray(np.block([[c1, -s1], [-s1, -c1]]), BF16)
    c2, s2 = _dft_cos_sin(n2)
    m2 = jnp.asarray(np.concatenate([c2, s2], axis=1), BF16)
    k1 = np.arange(n1, dtype=np.int64)[:, None]
    nn2 = np.arange(n2, dtype=np.int64)[None, :]
    ang = 2.0 * np.pi * ((k1 * nn2) % seq) / seq
    steps = n2 // n2_per_step
    tc = jnp.asarray(np.cos(ang).reshape(n1, steps, n2_per_step).transpose(1, 0, 2), F32)
    ts = jnp.asarray(np.sin(ang).reshape(n1, steps, n2_per_step).transpose(1, 0, 2), F32)

    cols = n2_per_step * ch
    dspec = pl.BlockSpec((n1, cols), lambda t: (0, t))
    tspec = pl.BlockSpec((1, n1, n2_per_step), lambda t: (t, 0, 0))
    tre, tim = pl.pallas_call(
        functools.partial(_fourier_stage1_kernel, n1=n1, n2_per_step=n2_per_step),
        grid=(steps,),
        in_specs=[pl.BlockSpec((2 * n1, 2 * n1), lambda t: (0, 0)), tspec, tspec, dspec, dspec],
        out_specs=[dspec, dspec],
        out_shape=[jax.ShapeDtypeStruct((n1, n2 * ch), BF16)] * 2,
        compiler_params=_params("parallel"),
        name="fourier_stage1",
    )(m1, tc, ts, y.reshape(n1, n2 * ch), z.reshape(n1, n2 * ch))

    tblock = pl.BlockSpec((k1_per_step * n2, ch), lambda t: (t, 0))
    out = pl.pallas_call(
        functools.partial(_fourier_stage2_kernel, n2=n2, k1_per_step=k1_per_step),
        grid=(n1 // k1_per_step,),
        in_specs=[pl.BlockSpec((n2, 2 * n2), lambda t: (0, 0)), tblock, tblock],
        out_specs=pl.BlockSpec((n2, k1_per_step * ch), lambda t: (0, t)),
        out_shape=jax.ShapeDtypeStruct((n2, n1 * ch), BF16),
        compiler_params=_params("parallel"),
        name="fourier_stage2",
    )(m2, tre.reshape(seq, ch), tim.reshape(seq, ch))
    return out.reshape(seq, ch)


def _diff_bias_kernel(tab_ref, o_ref, *, t):
    h = pl.program_id(0)
    row = lax.broadcasted_iota(jnp.int32, (t, t), 0)
    col = lax.broadcasted_iota(jnp.int32, (t, t), 1)
    base = row - col
    for r in range(2 * DIFF_R + 1):
        d = (r - DIFF_R) * t
        o_ref[0, r] = _bias_chain(base + d, d - (t - 1), d + (t - 1),
                                  lambda b: tab_ref[b, A_HEADS + h] * LOG2_E)


def _diff_bias(table, t):
    assert DIFF_R * t - (t - 1) >= _THR[_HALF_BUCKETS - 1]
    nt = 2 * DIFF_R + 1
    return pl.pallas_call(
        functools.partial(_diff_bias_kernel, t=t),
        grid=(C_HEADS,),
        in_specs=[pl.BlockSpec(memory_space=pltpu.SMEM)],
        out_specs=pl.BlockSpec((1, nt, t, t), lambda h: (h, 0, 0, 0)),
        out_shape=jax.ShapeDtypeStruct((C_HEADS, nt, t, t), F32),
        compiler_params=_params("arbitrary"),
        name="diff_bias",
    )(table)


def _diff_attn_kernel(q_ref, k_ref, v_ref, b_ref, lam_ref, g_ref, o_ref,
                      vt_sc, s_sc, p_sc, alpha_sc, m_sc, acc_sc, *, t, tk, lambda_init):
    qg = pl.program_id(1)
    nkv = k_ref.shape[0] // tk
    ntile = q_ref.shape[0] // t
    total = ntile * nkv
    per_tile = t // tk

    @pl.when(qg == 0)
    def _():
        pad = lax.broadcasted_iota(jnp.int32, (DIFF_VT_ROWS - C_V_DIM, tk), 0)
        ones_row = jnp.where(pad == 0, 1.0, 0.0).astype(BF16)
        for j in range(nkv):
            vt_sc[j, :C_V_DIM, :] = v_ref[tk * j:tk * (j + 1), :].astype(F32).T.astype(BF16)
            vt_sc[j, C_V_DIM:, :] = ones_row

    def split(n):
        return n // nkv, n % nkv

    def key_rows(j):
        return pl.ds(pl.multiple_of(j * tk, tk), tk)

    def query_rows(tile):
        return pl.ds(pl.multiple_of(tile * t, t), t)

    def logits(n, buf):
        tile, j = split(n)
        bias_tile = jnp.clip(j // per_tile - (qg * ntile + tile), -DIFF_R, DIFF_R) + DIFF_R
        bias = b_ref[0, bias_tile, pl.ds(pl.multiple_of((j % per_tile) * tk, tk), tk), :]
        for c in range(2):
            hs = slice(HEAD_DIM * c, HEAD_DIM * (c + 1))
            s_sc[buf, c] = _dot_nt(k_ref[key_rows(j), hs], q_ref[query_rows(tile), hs]) + bias

    def softmax(n, buf):
        _, j = split(n)
        for c in range(2):
            m_cur = jnp.max(s_sc[buf, c], axis=0, keepdims=True)
            m_prev = jnp.where(j == 0, NEG_INF, m_sc[c])
            m_next = jnp.maximum(m_prev, m_cur)
            alpha_sc[buf, c] = jnp.exp2(m_prev - m_next)
            m_sc[c] = m_next
            p_sc[buf, c] = jnp.exp2(s_sc[buf, c] - m_next).astype(BF16)

    def values(n, buf):
        _, j = split(n)
        for c in range(2):
            acc_sc[c] = acc_sc[c] * alpha_sc[buf, c] + _dot(vt_sc[j], p_sc[buf, c])

    def finalize(tile):
        lp = lam_ref[...]
        lam = (jnp.exp(jnp.sum(lp[0:1] * lp[1:2], axis=-1, keepdims=True))
               - jnp.exp(jnp.sum(lp[2:3] * lp[3:4], axis=-1, keepdims=True)) + lambda_init)
        num0, den0 = acc_sc[0, :C_V_DIM, :], acc_sc[0, C_V_DIM:C_V_DIM + 1, :]
        num1, den1 = acc_sc[1, :C_V_DIM, :], acc_sc[1, C_V_DIM:C_V_DIM + 1, :]
        o = num0 * (1.0 / den0) - lam * (num1 * (1.0 / den1))
        ms = jnp.mean(o * o, axis=0, keepdims=True)
        o = o * lax.rsqrt(ms + NORM_EPS) * (g_ref[...] * (1.0 - lambda_init))
        o_ref[query_rows(tile), :] = o.T.astype(o_ref.dtype)

    m_sc[...] = jnp.full(m_sc.shape, NEG_INF, F32)
    acc_sc[...] = jnp.zeros(acc_sc.shape, F32)

    logits(0, 0)
    logits(1, 1)
    softmax(0, 0)

    def body(i, carry):
        n = 2 * i + 1
        logits(n + 1, 0)
        softmax(n, 1)
        values(n - 1, 0)
        logits(n + 2, 1)
        softmax(n + 1, 0)
        values(n, 1)

        @pl.when((n + 1) % nkv == 0)
        def _():
            finalize(n // nkv)

        return carry

    lax.fori_loop(0, total // 2 - 1, body, 0)
    softmax(total - 1, 1)
    values(total - 2, 0)
    values(total - 1, 1)
    finalize(ntile - 1)


def _diff_attn(proj, bias, lam_params, subln_g, lambda_init, *, t, tk=DIFF_TK, tiles_per_step=DIFF_TILES_PER_STEP):
    s = proj.shape[0]
    nt = 2 * DIFF_R + 1
    tk = min(tk, t)
    nkv = s // tk
    tiles_per_step = min(tiles_per_step, s // t)
    tq = tiles_per_step * t
    assert nkv * tk == s and nkv % 2 == 0 and nkv >= 4 and t % tk == 0 and s % tq == 0
    return pl.pallas_call(
        functools.partial(_diff_attn_kernel, t=t, tk=tk, lambda_init=lambda_init),
        grid=(C_HEADS, s // tq),
        in_specs=[pl.BlockSpec((tq, C_V_DIM), lambda h, i: (i, h)),
                  pl.BlockSpec((s, C_V_DIM), lambda h, i: (0, C_HEADS + h)),
                  pl.BlockSpec((s, C_V_DIM), lambda h, i: (0, 2 * C_HEADS + h)),
                  pl.BlockSpec((1, nt, t, t), lambda h, i: (h, 0, 0, 0)),
                  pl.BlockSpec((4, HEAD_DIM), lambda h, i: (0, 0)),
                  pl.BlockSpec((C_V_DIM, 1), lambda h, i: (0, 0))],
        out_specs=pl.BlockSpec((tq, C_V_DIM), lambda h, i: (i, h)),
        out_shape=jax.ShapeDtypeStruct((s, C_HEADS * C_V_DIM), BF16),
        scratch_shapes=[pltpu.VMEM((nkv, DIFF_VT_ROWS, tk), BF16),
                        pltpu.VMEM((2, 2, tk, t), F32), pltpu.VMEM((2, 2, tk, t), BF16),
                        pltpu.VMEM((2, 2, 1, t), F32), pltpu.VMEM((2, 1, t), F32),
                        pltpu.VMEM((2, DIFF_VT_ROWS, t), F32)],
        compiler_params=_params("arbitrary", "arbitrary"),
        name="diff_attn",
    )(proj, proj, proj, bias, lam_params, subln_g.reshape(C_V_DIM, 1))


def kernel(x, norm_mix_g, norm_ffn_g, norm_final_g, rel_bias_table, w_in_even, w_fnet, w_out_even,
           w_qkv_odd, lambda_q1, lambda_k1, lambda_q2, lambda_k2, subln_g, w_out_odd, w_ff1, w_ff2):
    batch, seq, d_model = x.shape
    depth = norm_mix_g.shape[0]
    table = rel_bias_table.astype(F32)
    dil_bias = _dilated_bias(table)
    diff_t = min(DIFF_T, seq)
    diff_bias = _diff_bias(table, diff_t)
    c_qk_width = C_HEADS * 2 * HEAD_DIM
    odd_scale = jnp.concatenate([jnp.full((1, c_qk_width), LOG2_E / math.sqrt(HEAD_DIM), F32),
                                 jnp.ones((1, w_qkv_odd.shape[2] - c_qk_width), F32)], axis=1)
    even_scale = jnp.concatenate([jnp.full((1, A_WIDTH), LOG2_E / math.sqrt(HEAD_DIM), F32),
                                  jnp.ones((1, w_in_even.shape[2] - A_WIDTH), F32)], axis=1)

    outs = []
    for bidx in range(batch):
        xs = x[bidx]
        for i in range(depth):
            j = i // 2
            if i % 2 == 0:
                class_dils = tuple(dil for _, dil in DILATED_CONFIGS if dil > 1)
                proj, *by_class = _norm_matmul(xs, norm_mix_g[i], w_in_even[j].astype(BF16), even_scale,
                                               class_dils=class_dils)
                by_class = dict(zip(class_dils, by_class))
                by_class[1] = proj.reshape(1, *proj.shape)
                branches = [_dilated_branch(by_class[dil], dil_bias[bi])
                            for bi, (_, dil) in enumerate(DILATED_CONFIGS)]
                y, z = _fourier_channel(proj, _fourier_weights(w_fnet[j], seq))
                fb = _fourier_position(y, z)
                mixed = _dilated_merge([b[0] for b in branches], [b[1] for b in branches], fb)
                xs = _matmul_residual(mixed, w_out_even[j].astype(BF16), xs)
            else:
                lambda_init = 0.8 - 0.6 * math.exp(-0.3 * i)
                proj, = _norm_matmul(xs, norm_mix_g[i], w_qkv_odd[j].astype(BF16), odd_scale)
                lam_params = jnp.stack([lambda_q1[j], lambda_k1[j], lambda_q2[j], lambda_k2[j]]).astype(F32)
                attn = _diff_attn(proj, diff_bias, lam_params, subln_g[j], lambda_init, t=diff_t)
                xs = _matmul_residual(attn, w_out_odd[j].astype(BF16), xs)
            xs = _mlp(xs, norm_ffn_g[i], w_ff1[i].astype(BF16), w_ff2[i].astype(BF16), norm_final_g,
                      final_norm=(i == depth - 1))
        outs.append(xs)
    return jnp.stack(outs, axis=0)
```

```python
import functools
import math

import numpy as np
import jax
import jax.numpy as jnp
from jax import lax
from jax.experimental import pallas as pl
from jax.experimental.pallas import tpu as pltpu

F32 = jnp.float32
BF16 = jnp.bfloat16

HEAD_DIM = 128
A_HEADS = 12
B_GROUPS = 4
C_HEADS = 8
DILATED_CONFIGS = ((128, 1), (512, 4), (2048, 16))
NUM_BUCKETS = 32
MAX_DISTANCE = 1024
NORM_EPS = 1e-6
NEG_INF = -1e30
LOG2_E = math.log2(math.e)

A_WIDTH = A_HEADS * HEAD_DIM
B_WIDTH = B_GROUPS * HEAD_DIM
C_V_DIM = 2 * HEAD_DIM
HALF_STEPS = 64
assert all(w // (2 * d) == HALF_STEPS for w, d in DILATED_CONFIGS)

LANES = 128
V7X_VMEM_LIMIT_BYTES = 56 * 1024 * 1024

DIL_TQ = 512
DIL_SUB = 128
DIL_BAND = DIL_SUB + 2 * HALF_STEPS
DIL_HG = 4
DIFF_T = 512
DIFF_TK = 512
DIFF_TILES_PER_STEP = 8
DIFF_R = 3
BF16_SUBLANES = 16
DIFF_VT_ROWS = C_V_DIM + BF16_SUBLANES
FFT_N2 = 128


def _params(*sem, flags=None):
    return pltpu.CompilerParams(dimension_semantics=sem, vmem_limit_bytes=V7X_VMEM_LIMIT_BYTES, flags=flags)


def _magnitude_thresholds():
    half = NUM_BUCKETS // 2
    max_exact = half // 2
    n = np.arange(1, 4 * MAX_DISTANCE, dtype=np.int64)
    large = max_exact + (np.log(n / max_exact) / math.log(MAX_DISTANCE / max_exact)
                         * (half - max_exact)).astype(np.int64)
    large = np.minimum(large, half - 1)
    bucket = np.where(n < max_exact, n, large)
    assert np.all(np.diff(bucket) >= 0)
    thr = {b: int(n[np.argmax(bucket >= b)]) for b in range(1, half)}
    return thr


_THR = _magnitude_thresholds()
_HALF_BUCKETS = NUM_BUCKETS // 2


def _mag_bucket(n):
    return sum(1 for b in range(1, _HALF_BUCKETS) if n >= _THR[b])


def _bias_chain(rel, lo, hi, tab):
    n = jnp.abs(rel)

    def side(base, nlo, nhi):
        bmin, bmax = _mag_bucket(nlo), _mag_bucket(nhi)
        val = jnp.full(rel.shape, tab(base + bmax), F32)
        for b in range(bmax - 1, bmin - 1, -1):
            val = jnp.where(n < _THR[b + 1], tab(base + b), val)
        return val

    if lo > 0:
        return side(_HALF_BUCKETS, lo, hi)
    if hi <= 0:
        return side(0, -hi, -lo)
    return jnp.where(rel > 0, side(_HALF_BUCKETS, 1, hi), side(0, 0, -lo))


def _rms(xf, g):
    ms = jnp.mean(xf * xf, axis=-1, keepdims=True)
    return xf * lax.rsqrt(ms + NORM_EPS) * g


def _dot(a, b):
    return jnp.dot(a, b, preferred_element_type=F32)


def _dot_nt(a, b):
    return lax.dot_general(a, b, (((1,), (1,)), ((), ())), preferred_element_type=F32)


def _norm_matmul_kernel(x_ref, g_ref, w_ref, cs_ref, o_ref, *rest, class_dils):
    class_refs = rest[:len(class_dils)]
    hn_ref = rest[len(class_dils)]

    @pl.when(pl.program_id(1) == 0)
    def _():
        hn_ref[...] = _rms(x_ref[...], g_ref[...]).astype(BF16)

    y = _dot(hn_ref[...], w_ref[...]) * cs_ref[...]
    o_ref[...] = y.astype(o_ref.dtype)

    if class_dils:
        y_sc = rest[len(class_dils) + 1]
        tm, tn = y.shape
        for c in range(tn // LANES):
            y_sc[c] = y[:, LANES * c:LANES * (c + 1)]
        for ref, dil in zip(class_refs, class_dils):
            for r in range(dil):
                for c in range(tn // LANES):
                    ref[r, :, LANES * c:LANES * (c + 1)] = (
                        y_sc[c, pl.ds(r, tm // dil, stride=dil), :].astype(ref.dtype))


def _norm_matmul(x, g, w, col_scale, *, class_dils=(), tm=1024, tn=1024):
    m, d = x.shape
    n = w.shape[1]
    tm = min(tm, m)
    out_specs = [pl.BlockSpec((tm, tn), lambda i, j: (i, j))]
    out_shape = [jax.ShapeDtypeStruct((m, n), BF16)]
    scratch = [pltpu.VMEM((tm, d), BF16)]
    for dil in class_dils:
        out_specs.append(pl.BlockSpec((dil, tm // dil, tn), lambda i, j: (0, i, j)))
        out_shape.append(jax.ShapeDtypeStruct((dil, m // dil, n), BF16))
    if class_dils:
        scratch.append(pltpu.VMEM((tn // LANES, tm, LANES), F32))
    return pl.pallas_call(
        functools.partial(_norm_matmul_kernel, class_dils=tuple(class_dils)),
        grid=(m // tm, n // tn),
        in_specs=[pl.BlockSpec((tm, d), lambda i, j: (i, 0)),
                  pl.BlockSpec((1, d), lambda i, j: (0, 0)),
                  pl.BlockSpec((d, tn), lambda i, j: (0, j)),
                  pl.BlockSpec((1, tn), lambda i, j: (0, j))],
        out_specs=out_specs,
        out_shape=out_shape,
        scratch_shapes=scratch,
        compiler_params=_params("parallel", "arbitrary"),
        name="norm_matmul",
    )(x, g.reshape(1, d), w, col_scale)


def _matmul_residual_kernel(a_ref, w_ref, r_ref, o_ref):
    o_ref[...] = r_ref[...] + _dot(a_ref[...], w_ref[...])


def _matmul_residual(a, w, res, *, tm=512):
    m, k = a.shape
    n = w.shape[1]
    tm = min(tm, m)
    return pl.pallas_call(
        _matmul_residual_kernel,
        grid=(m // tm,),
        in_specs=[pl.BlockSpec((tm, k), lambda i: (i, 0)),
                  pl.BlockSpec((k, n), lambda i: (0, 0)),
                  pl.BlockSpec((tm, n), lambda i: (i, 0))],
        out_specs=pl.BlockSpec((tm, n), lambda i: (i, 0)),
        out_shape=jax.ShapeDtypeStruct((m, n), F32),
        compiler_params=_params("parallel"),
        name="matmul_residual",
    )(a, w, res)


def _mlp_kernel(x_ref, g_ref, w1_ref, w2_ref, gf_ref, o_ref, hn_ref, *, final_norm):
    c = pl.program_id(1)

    @pl.when(c == 0)
    def _():
        x = x_ref[...]
        hn_ref[...] = _rms(x, g_ref[...]).astype(BF16)
        o_ref[...] = x

    u = jnp.maximum(_dot(hn_ref[...], w1_ref[...]), 0.0)
    o_ref[...] += _dot((u * u).astype(BF16), w2_ref[...])

    if final_norm:
        @pl.when(c == pl.num_programs(1) - 1)
        def _():
            o_ref[...] = _rms(o_ref[...], gf_ref[...])


def _mlp(x, g, w1, w2, gf, *, final_norm, tm=1024, tf=512):
    m, d = x.shape
    f = w1.shape[1]
    tm = min(tm, m)
    return pl.pallas_call(
        functools.partial(_mlp_kernel, final_norm=final_norm),
        grid=(m // tm, f // tf),
        in_specs=[pl.BlockSpec((tm, d), lambda i, c: (i, 0)),
                  pl.BlockSpec((1, d), lambda i, c: (0, 0)),
                  pl.BlockSpec((d, tf), lambda i, c: (0, c)),
                  pl.BlockSpec((tf, d), lambda i, c: (c, 0)),
                  pl.BlockSpec((1, d), lambda i, c: (0, 0))],
        out_specs=pl.BlockSpec((tm, d), lambda i, c: (i, 0)),
        out_shape=jax.ShapeDtypeStruct((m, d), F32),
        scratch_shapes=[pltpu.VMEM((tm, d), BF16)],
        compiler_params=_params("parallel", "arbitrary"),
        name="mlp",
    )(x, g.reshape(1, d), w1, w2, gf.reshape(1, d))


def _dilated_bias_kernel(tab_ref, o_ref):
    h = pl.program_id(0)
    row = lax.broadcasted_iota(jnp.int32, (DIL_SUB, DIL_BAND), 0)
    col = lax.broadcasted_iota(jnp.int32, (DIL_SUB, DIL_BAND), 1)
    steps = col - row - HALF_STEPS
    in_band = jnp.abs(steps) <= HALF_STEPS
    for bi, (_, dil) in enumerate(DILATED_CONFIGS):
        reach = HALF_STEPS * dil
        rel = jnp.clip(steps, -HALF_STEPS, HALF_STEPS) * dil
        bias = _bias_chain(rel, -reach, reach, lambda b: tab_ref[b, h] * LOG2_E)
        o_ref[bi, 0] = jnp.where(in_band, bias, NEG_INF)


def _dilated_bias(table):
    nb = len(DILATED_CONFIGS)
    return pl.pallas_call(
        _dilated_bias_kernel,
        grid=(A_HEADS,),
        in_specs=[pl.BlockSpec(memory_space=pltpu.SMEM)],
        out_specs=pl.BlockSpec((nb, 1, DIL_SUB, DIL_BAND), lambda h: (0, h, 0, 0)),
        out_shape=jax.ShapeDtypeStruct((nb, A_HEADS, DIL_SUB, DIL_BAND), F32),
        compiler_params=_params("arbitrary"),
        name="dilated_bias",
    )(table)


def _dilated_branch_kernel(q_ref, kp_ref, kc_ref, kn_ref, vp_ref, vc_ref, vn_ref, b_ref,
                           o_ref, lse_ref, *, tq):
    tb = pl.program_id(2)
    last_tb = pl.num_programs(2) - 1
    nsub = tq // DIL_SUB
    col = lax.broadcasted_iota(jnp.int32, (DIL_SUB, DIL_BAND), 1)
    lane = lax.broadcasted_iota(jnp.int32, (DIL_SUB, LANES), 1)
    lanes_per_head = LANES // DIL_HG

    def band(prev_ref, cur_ref, next_ref, i, hs):
        lo = DIL_SUB * i - HALF_STEPS
        hi = lo + DIL_BAND
        parts = []
        if lo < 0:
            parts.append(prev_ref[:, hs])
        parts.append(cur_ref[max(lo, 0):min(hi, tq), hs])
        if hi > tq:
            parts.append(next_ref[:, hs])
        return parts[0] if len(parts) == 1 else jnp.concatenate(parts, axis=0)

    for i in range(nsub):
        lse_blk = None
        for j in range(DIL_HG):
            hs = slice(HEAD_DIM * j, HEAD_DIM * (j + 1))
            q = q_ref[DIL_SUB * i:DIL_SUB * (i + 1), hs]
            kb = band(kp_ref, kc_ref, kn_ref, i, hs)
            vb = band(vp_ref, vc_ref, vn_ref, i, hs)
            s = _dot_nt(q, kb) + b_ref[j]
            if i == 0:
                s = jnp.where((tb > 0) | (col >= HALF_STEPS), s, NEG_INF)
            if i == nsub - 1:
                s = jnp.where((tb < last_tb) | (col < DIL_BAND - HALF_STEPS), s, NEG_INF)
            m = jnp.max(s, axis=-1, keepdims=True)
            p = jnp.exp2(s - m)
            den = jnp.sum(p, axis=-1, keepdims=True)
            o = _dot(p.astype(BF16), vb) / den
            o_ref[DIL_SUB * i:DIL_SUB * (i + 1), hs] = o.astype(o_ref.dtype)
            lse = m + jnp.log2(den)
            lse_blk = lse if lse_blk is None else jnp.where(lane >= lanes_per_head * j, lse, lse_blk)
        lse_ref[DIL_SUB * i:DIL_SUB * (i + 1), :] = jnp.broadcast_to(lse_blk, (DIL_SUB, LANES))


def _dilated_branch(proj_c, bias_b, *, tq=DIL_TQ):
    dil, l, w = proj_c.shape
    tq = min(tq, l)
    ngroups = A_HEADS // DIL_HG
    gw = DIL_HG * HEAD_DIM
    qoff, koff, voff = 0, A_WIDTH // gw, 2 * A_WIDTH // gw
    halo_per_tq = tq // HALF_STEPS
    n_halo = l // HALF_STEPS

    def cur(off):
        return pl.BlockSpec((None, tq, gw), lambda r, g, t: (r, t, off + g))

    def prev(off):
        return pl.BlockSpec((None, HALF_STEPS, gw),
                            lambda r, g, t: (r, jnp.maximum(t * halo_per_tq - 1, 0), off + g))

    def nxt(off):
        return pl.BlockSpec((None, HALF_STEPS, gw),
                            lambda r, g, t: (r, jnp.minimum((t + 1) * halo_per_tq, n_halo - 1), off + g))

    return pl.pallas_call(
        functools.partial(_dilated_branch_kernel, tq=tq),
        grid=(dil, ngroups, l // tq),
        in_specs=[cur(qoff), prev(koff), cur(koff), nxt(koff), prev(voff), cur(voff), nxt(voff),
                  pl.BlockSpec((DIL_HG, DIL_SUB, DIL_BAND), lambda r, g, t: (g, 0, 0))],
        out_specs=[pl.BlockSpec((None, tq, gw), lambda r, g, t: (r, t, g)),
                   pl.BlockSpec((None, tq, LANES), lambda r, g, t: (r, t, g))],
        out_shape=[jax.ShapeDtypeStruct((dil, l, A_WIDTH), BF16),
                   jax.ShapeDtypeStruct((dil, l, ngroups * LANES), F32)],
        compiler_params=_params("parallel", "parallel", "arbitrary"),
        name=f"dilated_branch_d{dil}",
    )(*([proj_c] * 7), bias_b)


def _dilated_merge_kernel(*refs, dils):
    nb = len(dils)
    o_refs, l_refs = refs[:nb], refs[nb:2 * nb]
    fb_ref, out_ref, l_sc, o_sc = refs[2 * nb:]
    g = pl.program_id(1)
    ngroups = A_HEADS // DIL_HG
    lanes_per_head = LANES // DIL_HG
    tm = out_ref.shape[0]

    @pl.when(g < ngroups)
    def _():
        for b, dil in enumerate(dils):
            for r in range(dil):
                rows = pl.ds(r, tm // dil, stride=dil)
                l_sc[b, rows, :] = l_refs[b][r]
                for j in range(DIL_HG):
                    o_sc[b, j, rows, :] = o_refs[b][r, :, HEAD_DIM * j:HEAD_DIM * (j + 1)].astype(F32)
        ls = [l_sc[b] for b in range(nb)]
        mx = functools.reduce(jnp.maximum, ls)
        es = [jnp.exp2(l - mx) for l in ls]
        inv = 1.0 / functools.reduce(jnp.add, es)
        for j in range(DIL_HG):
            lc = slice(lanes_per_head * j, lanes_per_head * j + 1)
            acc = functools.reduce(jnp.add, [(es[b] * inv)[:, lc] * o_sc[b, j] for b in range(nb)])
            out_ref[:, HEAD_DIM * j:HEAD_DIM * (j + 1)] = acc.astype(out_ref.dtype)

    @pl.when(g == ngroups)
    def _():
        out_ref[...] = fb_ref[...]


def _dilated_merge(os_, lses, fb, *, tm=512):
    s = fb.shape[0]
    tm = min(tm, s)
    ngroups = A_HEADS // DIL_HG
    gw = DIL_HG * HEAD_DIM
    assert fb.shape[1] == gw and HEAD_DIM == LANES
    dils = tuple(o.shape[0] for o in os_)
    clamp = lambda i, g: (0, i, jnp.minimum(g, ngroups - 1))
    return pl.pallas_call(
        functools.partial(_dilated_merge_kernel, dils=dils),
        grid=(s // tm, ngroups + 1),
        in_specs=[pl.BlockSpec((d, tm // d, gw), clamp) for d in dils]
                 + [pl.BlockSpec((d, tm // d, LANES), clamp) for d in dils]
                 + [pl.BlockSpec((tm, gw), lambda i, g: (i, 0))],
        out_specs=pl.BlockSpec((tm, gw), lambda i, g: (i, g)),
        out_shape=jax.ShapeDtypeStruct((s, A_WIDTH + B_WIDTH), BF16),
        scratch_shapes=[pltpu.VMEM((len(dils), tm, LANES), F32),
                        pltpu.VMEM((len(dils), DIL_HG, tm, LANES), F32)],
        compiler_params=_params("parallel", "arbitrary"),
        name="dilated_merge",
    )(*os_, *lses, fb)


def _dft_cos_sin(n):
    idx = np.arange(n, dtype=np.int64)
    ang = 2.0 * np.pi * ((idx[:, None] * idx[None, :]) % n) / n
    return np.cos(ang), np.sin(ang)


def _fourier_weights_kernel(cs_ref, w_ref, o_ref, *, norm):
    w = w_ref[0]
    ab = jnp.dot(cs_ref[...], w, preferred_element_type=F32, precision=lax.Precision.HIGHEST) * norm
    o_ref[0] = jnp.concatenate([ab[:HEAD_DIM], ab[HEAD_DIM:]], axis=1).astype(o_ref.dtype)


def _fourier_weights(w_f, seq):
    c, s = _dft_cos_sin(HEAD_DIM)
    cs = jnp.asarray(np.concatenate([c, s], axis=0), F32)
    norm = 1.0 / math.sqrt(seq * HEAD_DIM)
    return pl.pallas_call(
        functools.partial(_fourier_weights_kernel, norm=norm),
        grid=(B_GROUPS,),
        in_specs=[pl.BlockSpec((2 * HEAD_DIM, HEAD_DIM), lambda g: (0, 0)),
                  pl.BlockSpec((1, HEAD_DIM, HEAD_DIM), lambda g: (g, 0, 0))],
        out_specs=pl.BlockSpec((1, HEAD_DIM, 2 * HEAD_DIM), lambda g: (g, 0, 0)),
        out_shape=jax.ShapeDtypeStruct((B_GROUPS, HEAD_DIM, 2 * HEAD_DIM), BF16),
        compiler_params=_params("arbitrary"),
        name="fourier_weights",
    )(cs, w_f)


def _fourier_channel_kernel(u_ref, ab_ref, y_ref, z_ref):
    for g in range(B_GROUPS):
        hs = slice(HEAD_DIM * g, HEAD_DIM * (g + 1))
        yz = _dot(u_ref[:, hs], ab_ref[g])
        y_ref[:, hs] = yz[:, :HEAD_DIM].astype(y_ref.dtype)
        z_ref[:, hs] = yz[:, HEAD_DIM:].astype(z_ref.dtype)


def _fourier_channel(proj, ab, *, tm=1024):
    s, w = proj.shape
    ublock = (w - B_WIDTH) // B_WIDTH
    assert ublock * B_WIDTH == w - B_WIDTH
    spec = pl.BlockSpec((tm, B_WIDTH), lambda i: (i, 0))
    return pl.pallas_call(
        _fourier_channel_kernel,
        grid=(s // tm,),
        in_specs=[pl.BlockSpec((tm, B_WIDTH), lambda i: (i, ublock)),
                  pl.BlockSpec((B_GROUPS, HEAD_DIM, 2 * HEAD_DIM), lambda i: (0, 0, 0))],
        out_specs=[spec, spec],
        out_shape=[jax.ShapeDtypeStruct((s, B_WIDTH), BF16)] * 2,
        compiler_params=_params("parallel"),
        name="fourier_channel",
    )(proj, ab)


def _fourier_stage1_kernel(m1_ref, tc_ref, ts_ref, y_ref, z_ref, tre_ref, tim_ref, *, n1, n2_per_step):
    yz = jnp.concatenate([y_ref[...], z_ref[...]], axis=0)
    ab = _dot(m1_ref[...], yz)
    a, b = ab[:n1], ab[n1:]
    ch = B_WIDTH
    for q in range(n2_per_step):
        cs = slice(ch * q, ch * (q + 1))
        c = tc_ref[0, :, q:q + 1]
        s = ts_ref[0, :, q:q + 1]
        aq, bq = a[:, cs], b[:, cs]
        tre_ref[:, cs] = (aq * c + bq * s).astype(tre_ref.dtype)
        tim_ref[:, cs] = (bq * c - aq * s).astype(tim_ref.dtype)


def _fourier_stage2_kernel(m2_ref, tre_ref, tim_ref, o_ref, *, n2, k1_per_step):
    ch = B_WIDTH
    for q in range(k1_per_step):
        rs = slice(n2 * q, n2 * (q + 1))
        t = jnp.concatenate([tre_ref[rs, :], tim_ref[rs, :]], axis=0)
        o_ref[:, ch * q:ch * (q + 1)] = _dot(m2_ref[...], t).astype(o_ref.dtype)


def _fourier_position(y, z, *, n2=FFT_N2, n2_per_step=16, k1_per_step=4):
    seq, ch = y.shape
    n1 = seq // n2
    n2_per_step = min(n2_per_step, n2)
    k1_per_step = min(k1_per_step, n1)
    c1, s1 = _dft_cos_sin(n1)
    m1 = jnp.asarray(np.block([[c1, -s1], [-s1, -c1]]), BF16)
    c2, s2 = _dft_cos_sin(n2)
    m2 = jnp.asarray(np.concatenate([c2, s2], axis=1), BF16)
    k1 = np.arange(n1, dtype=np.int64)[:, None]
    nn2 = np.arange(n2, dtype=np.int64)[None, :]
    ang = 2.0 * np.pi * ((k1 * nn2) % seq) / seq
    steps = n2 // n2_per_step
    tc = jnp.asarray(np.cos(ang).reshape(n1, steps, n2_per_step).transpose(1, 0, 2), F32)
    ts = jnp.asarray(np.sin(ang).reshape(n1, steps, n2_per_step).transpose(1, 0, 2), F32)

    cols = n2_per_step * ch
    dspec = pl.BlockSpec((n1, cols), lambda t: (0, t))
    tspec = pl.BlockSpec((1, n1, n2_per_step), lambda t: (t, 0, 0))
    tre, tim = pl.pallas_call(
        functools.partial(_fourier_stage1_kernel, n1=n1, n2_per_step=n2_per_step),
        grid=(steps,),
        in_specs=[pl.BlockSpec((2 * n1, 2 * n1), lambda t: (0, 0)), tspec, tspec, dspec, dspec],
        out_specs=[dspec, dspec],
        out_shape=[jax.ShapeDtypeStruct((n1, n2 * ch), BF16)] * 2,
        compiler_params=_params("parallel"),
        name="fourier_stage1",
    )(m1, tc, ts, y.reshape(n1, n2 * ch), z.reshape(n1, n2 * ch))

    tblock = pl.BlockSpec((k1_per_step * n2, ch), lambda t: (t, 0))
    out = pl.pallas_call(
        functools.partial(_fourier_stage2_kernel, n2=n2, k1_per_step=k1_per_step),
        grid=(n1 // k1_per_step,),
        in_specs=[pl.BlockSpec((n2, 2 * n2), lambda t: (0, 0)), tblock, tblock],
        out_specs=pl.BlockSpec((n2, k1_per_step * ch), lambda t: (0, t)),
        out_shape=jax.ShapeDtypeStruct((n2, n1 * ch), BF16),
        compiler_params=_params("parallel"),
        name="fourier_stage2",
    )(m2, tre.reshape(seq, ch), tim.reshape(seq, ch))
    return out.reshape(seq, ch)


def _diff_bias_kernel(tab_ref, o_ref, *, t):
    h = pl.program_id(0)
    row = lax.broadcasted_iota(jnp.int32, (t, t), 0)
    col = lax.broadcasted_iota(jnp.int32, (t, t), 1)
    base = row - col
    for r in range(2 * DIFF_R + 1):
        d = (r - DIFF_R) * t
        o_ref[0, r] = _bias_chain(base + d, d - (t - 1), d + (t - 1),
                                  lambda b: tab_ref[b, A_HEADS + h] * LOG2_E)


def _diff_bias(table, t):
    assert DIFF_R * t - (t - 1) >= _THR[_HALF_BUCKETS - 1]
    nt = 2 * DIFF_R + 1
    return pl.pallas_call(
        functools.partial(_diff_bias_kernel, t=t),
        grid=(C_HEADS,),
        in_specs=[pl.BlockSpec(memory_space=pltpu.SMEM)],
        out_specs=pl.BlockSpec((1, nt, t, t), lambda h: (h, 0, 0, 0)),
        out_shape=jax.ShapeDtypeStruct((C_HEADS, nt, t, t), F32),
        compiler_params=_params("arbitrary"),
        name="diff_bias",
    )(table)


def _diff_attn_kernel(q_ref, k_ref, v_ref, b_ref, lam_ref, g_ref, o_ref,
                      vt_sc, s0_sc, s1_sc, p0_sc, p1_sc, alpha0_sc, alpha1_sc, m_sc, acc_sc,
                      *, t, tk, lambda_init):
    s_sc, p_sc, alpha_sc = (s0_sc, s1_sc), (p0_sc, p1_sc), (alpha0_sc, alpha1_sc)
    qg = pl.program_id(1)
    nkv = k_ref.shape[0] // tk
    ntile = q_ref.shape[0] // t
    total = ntile * nkv
    per_tile = t // tk

    @pl.when(qg == 0)
    def _():
        pad = lax.broadcasted_iota(jnp.int32, (DIFF_VT_ROWS - C_V_DIM, tk), 0)
        ones_row = jnp.where(pad == 0, 1.0, 0.0).astype(BF16)
        for j in range(nkv):
            vt_sc[j, :C_V_DIM, :] = v_ref[tk * j:tk * (j + 1), :].astype(F32).T.astype(BF16)
            vt_sc[j, C_V_DIM:, :] = ones_row

    def split(n):
        return n // nkv, n % nkv

    def key_rows(j):
        return pl.ds(pl.multiple_of(j * tk, tk), tk)

    def query_rows(tile):
        return pl.ds(pl.multiple_of(tile * t, t), t)

    def logits(n, buf):
        tile, j = split(n)
        q_tile = qg * ntile + tile
        for c in range(2):
            hs = slice(HEAD_DIM * c, HEAD_DIM * (c + 1))
            q = q_ref[query_rows(tile), hs]
            if tk <= t:
                bias_tile = jnp.clip(j // per_tile - q_tile, -DIFF_R, DIFF_R) + DIFF_R
                bias = b_ref[0, bias_tile, pl.ds(pl.multiple_of((j % per_tile) * tk, tk), tk), :]
                s_sc[buf][c] = _dot_nt(k_ref[key_rows(j), hs], q) + bias
            else:
                for r in range(tk // t):
                    bias_tile = jnp.clip(j * (tk // t) + r - q_tile, -DIFF_R, DIFF_R) + DIFF_R
                    k_rows = pl.ds(pl.multiple_of(j * tk + r * t, t), t)
                    s_sc[buf][c, t * r:t * (r + 1), :] = _dot_nt(k_ref[k_rows, hs], q) + b_ref[0, bias_tile]

    def softmax(n, buf):
        _, j = split(n)
        for c in range(2):
            m_cur = jnp.max(s_sc[buf][c], axis=0, keepdims=True)
            m_prev = jnp.where(j == 0, NEG_INF, m_sc[c])
            m_next = jnp.maximum(m_prev, m_cur)
            alpha_sc[buf][c] = jnp.exp2(m_prev - m_next)
            m_sc[c] = m_next
            p_sc[buf][c] = jnp.exp2(s_sc[buf][c] - m_next).astype(BF16)

    def values(n, buf):
        _, j = split(n)
        for c in range(2):
            acc_sc[c] = acc_sc[c] * alpha_sc[buf][c] + _dot(vt_sc[j], p_sc[buf][c])

    def finalize(tile):
        lp = lam_ref[...]
        lam = (jnp.exp(jnp.sum(lp[0:1] * lp[1:2], axis=-1, keepdims=True))
               - jnp.exp(jnp.sum(lp[2:3] * lp[3:4], axis=-1, keepdims=True)) + lambda_init)
        num0, den0 = acc_sc[0, :C_V_DIM, :], acc_sc[0, C_V_DIM:C_V_DIM + 1, :]
        num1, den1 = acc_sc[1, :C_V_DIM, :], acc_sc[1, C_V_DIM:C_V_DIM + 1, :]
        o = num0 * (1.0 / den0) - lam * (num1 * (1.0 / den1))
        ms = jnp.mean(o * o, axis=0, keepdims=True)
        o = o * lax.rsqrt(ms + NORM_EPS) * (g_ref[...] * (1.0 - lambda_init))
        o_ref[query_rows(tile), :] = o.T.astype(o_ref.dtype)

    m_sc[...] = jnp.full(m_sc.shape, NEG_INF, F32)
    acc_sc[...] = jnp.zeros(acc_sc.shape, F32)

    logits(0, 0)
    logits(1, 1)
    softmax(0, 0)

    def body(i, carry):
        n = 2 * i + 1
        logits(n + 1, 0)
        softmax(n, 1)
        values(n - 1, 0)
        logits(n + 2, 1)
        softmax(n + 1, 0)
        values(n, 1)

        @pl.when((n + 1) % nkv == 0)
        def _():
            finalize(n // nkv)

        return carry

    lax.fori_loop(0, total // 2 - 1, body, 0)
    softmax(total - 1, 1)
    values(total - 2, 0)
    values(total - 1, 1)
    finalize(ntile - 1)


def _diff_attn(proj, bias, lam_params, subln_g, lambda_init, *, t, tk=DIFF_TK, tiles_per_step=DIFF_TILES_PER_STEP):
    s = proj.shape[0]
    nt = 2 * DIFF_R + 1
    tk = min(tk, s // 4)
    nkv = s // tk
    tiles_per_step = min(tiles_per_step, s // t)
    tq = tiles_per_step * t
    assert nkv * tk == s and nkv % 2 == 0 and nkv >= 4 and (t % tk == 0 or tk % t == 0) and s % tq == 0
    return pl.pallas_call(
        functools.partial(_diff_attn_kernel, t=t, tk=tk, lambda_init=lambda_init),
        grid=(C_HEADS, s // tq),
        in_specs=[pl.BlockSpec((tq, C_V_DIM), lambda h, i: (i, h)),
                  pl.BlockSpec((s, C_V_DIM), lambda h, i: (0, C_HEADS + h)),
                  pl.BlockSpec((s, C_V_DIM), lambda h, i: (0, 2 * C_HEADS + h)),
                  pl.BlockSpec((1, nt, t, t), lambda h, i: (h, 0, 0, 0)),
                  pl.BlockSpec((4, HEAD_DIM), lambda h, i: (0, 0)),
                  pl.BlockSpec((C_V_DIM, 1), lambda h, i: (0, 0))],
        out_specs=pl.BlockSpec((tq, C_V_DIM), lambda h, i: (i, h)),
        out_shape=jax.ShapeDtypeStruct((s, C_HEADS * C_V_DIM), BF16),
        scratch_shapes=[pltpu.VMEM((nkv, DIFF_VT_ROWS, tk), BF16),
                        pltpu.VMEM((2, tk, t), F32), pltpu.VMEM((2, tk, t), F32),
                        pltpu.VMEM((2, tk, t), BF16), pltpu.VMEM((2, tk, t), BF16),
                        pltpu.VMEM((2, 1, t), F32), pltpu.VMEM((2, 1, t), F32), pltpu.VMEM((2, 1, t), F32),
                        pltpu.VMEM((2, DIFF_VT_ROWS, t), F32)],
        compiler_params=_params("arbitrary", "arbitrary"),
        name="diff_attn",
    )(proj, proj, proj, bias, lam_params, subln_g.reshape(C_V_DIM, 1))


def kernel(x, norm_mix_g, norm_ffn_g, norm_final_g, rel_bias_table, w_in_even, w_fnet, w_out_even,
           w_qkv_odd, lambda_q1, lambda_k1, lambda_q2, lambda_k2, subln_g, w_out_odd, w_ff1, w_ff2):
    batch, seq, d_model = x.shape
    depth = norm_mix_g.shape[0]
    table = rel_bias_table.astype(F32)
    dil_bias = _dilated_bias(table)
    diff_t = min(DIFF_T, seq)
    diff_bias = _diff_bias(table, diff_t)
    c_qk_width = C_HEADS * 2 * HEAD_DIM
    odd_scale = jnp.concatenate([jnp.full((1, c_qk_width), LOG2_E / math.sqrt(HEAD_DIM), F32),
                                 jnp.ones((1, w_qkv_odd.shape[2] - c_qk_width), F32)], axis=1)
    even_scale = jnp.concatenate([jnp.full((1, A_WIDTH), LOG2_E / math.sqrt(HEAD_DIM), F32),
                                  jnp.ones((1, w_in_even.shape[2] - A_WIDTH), F32)], axis=1)

    outs = []
    for bidx in range(batch):
        xs = x[bidx]
        for i in range(depth):
            j = i // 2
            if i % 2 == 0:
                class_dils = tuple(dil for _, dil in DILATED_CONFIGS if dil > 1)
                proj, *by_class = _norm_matmul(xs, norm_mix_g[i], w_in_even[j].astype(BF16), even_scale,
                                               class_dils=class_dils)
                by_class = dict(zip(class_dils, by_class))
                by_class[1] = proj.reshape(1, *proj.shape)
                branches = [_dilated_branch(by_class[dil], dil_bias[bi])
                            for bi, (_, dil) in enumerate(DILATED_CONFIGS)]
                y, z = _fourier_channel(proj, _fourier_weights(w_fnet[j], seq))
                fb = _fourier_position(y, z)
                mixed = _dilated_merge([b[0] for b in branches], [b[1] for b in branches], fb)
                xs = _matmul_residual(mixed, w_out_even[j].astype(BF16), xs)
            else:
                lambda_init = 0.8 - 0.6 * math.exp(-0.3 * i)
                proj, = _norm_matmul(xs, norm_mix_g[i], w_qkv_odd[j].astype(BF16), odd_scale)
                lam_params = jnp.stack([lambda_q1[j], lambda_k1[j], lambda_q2[j], lambda_k2[j]]).astype(F32)
                attn = _diff_attn(proj, diff_bias, lam_params, subln_g[j], lambda_init, t=diff_t)
                xs = _matmul_residual(attn, w_out_odd[j].astype(BF16), xs)
            xs = _mlp(xs, norm_ffn_g[i], w_ff1[i].astype(BF16), w_ff2[i].astype(BF16), norm_final_g,
                      final_norm=(i == depth - 1))
        outs.append(xs)
    return jnp.stack(outs, axis=0)
```

```python
import functools
import math

import numpy as np
import jax
import jax.numpy as jnp
from jax import lax
from jax.experimental import pallas as pl
from jax.experimental.pallas import tpu as pltpu

F32 = jnp.float32
BF16 = jnp.bfloat16

HEAD_DIM = 128
A_HEADS = 12
B_GROUPS = 4
C_HEADS = 8
DILATED_CONFIGS = ((128, 1), (512, 4), (2048, 16))
NUM_BUCKETS = 32
MAX_DISTANCE = 1024
NORM_EPS = 1e-6
NEG_INF = -1e30
LOG2_E = math.log2(math.e)

A_WIDTH = A_HEADS * HEAD_DIM
B_WIDTH = B_GROUPS * HEAD_DIM
C_V_DIM = 2 * HEAD_DIM
HALF_STEPS = 64
assert all(w // (2 * d) == HALF_STEPS for w, d in DILATED_CONFIGS)

LANES = 128
V7X_VMEM_LIMIT_BYTES = 56 * 1024 * 1024

DIL_TQ = 1024
DIL_SUB = 128
DIL_BAND = DIL_SUB + 2 * HALF_STEPS
DIL_HG = 4
DIFF_T = 512
DIFF_TK = 512
DIFF_TILES_PER_STEP = 8
DIFF_R = 3
BF16_SUBLANES = 16
DIFF_VT_ROWS = C_V_DIM + BF16_SUBLANES
FFT_N2 = 128


def _params(*sem, flags=None):
    return pltpu.CompilerParams(dimension_semantics=sem, vmem_limit_bytes=V7X_VMEM_LIMIT_BYTES, flags=flags)


def _magnitude_thresholds():
    half = NUM_BUCKETS // 2
    max_exact = half // 2
    n = np.arange(1, 4 * MAX_DISTANCE, dtype=np.int64)
    large = max_exact + (np.log(n / max_exact) / math.log(MAX_DISTANCE / max_exact)
                         * (half - max_exact)).astype(np.int64)
    large = np.minimum(large, half - 1)
    bucket = np.where(n < max_exact, n, large)
    assert np.all(np.diff(bucket) >= 0)
    thr = {b: int(n[np.argmax(bucket >= b)]) for b in range(1, half)}
    return thr


_THR = _magnitude_thresholds()
_HALF_BUCKETS = NUM_BUCKETS // 2


def _mag_bucket(n):
    return sum(1 for b in range(1, _HALF_BUCKETS) if n >= _THR[b])


def _bias_chain(rel, lo, hi, tab):
    n = jnp.abs(rel)

    def side(base, nlo, nhi):
        bmin, bmax = _mag_bucket(nlo), _mag_bucket(nhi)
        val = jnp.full(rel.shape, tab(base + bmax), F32)
        for b in range(bmax - 1, bmin - 1, -1):
            val = jnp.where(n < _THR[b + 1], tab(base + b), val)
        return val

    if lo > 0:
        return side(_HALF_BUCKETS, lo, hi)
    if hi <= 0:
        return side(0, -hi, -lo)
    return jnp.where(rel > 0, side(_HALF_BUCKETS, 1, hi), side(0, 0, -lo))


def _rms(xf, g):
    ms = jnp.mean(xf * xf, axis=-1, keepdims=True)
    return xf * lax.rsqrt(ms + NORM_EPS) * g


def _dot(a, b):
    return jnp.dot(a, b, preferred_element_type=F32)


def _dot_nt(a, b):
    return lax.dot_general(a, b, (((1,), (1,)), ((), ())), preferred_element_type=F32)


def _norm_matmul_kernel(x_ref, g_ref, w_ref, cs_ref, o_ref, *rest, class_dils):
    class_refs = rest[:len(class_dils)]
    hn_ref = rest[len(class_dils)]

    @pl.when(pl.program_id(1) == 0)
    def _():
        hn_ref[...] = _rms(x_ref[...], g_ref[...]).astype(BF16)

    y = _dot(hn_ref[...], w_ref[...]) * cs_ref[...]
    o_ref[...] = y.astype(o_ref.dtype)

    if class_dils:
        y_sc = rest[len(class_dils) + 1]
        tm, tn = y.shape
        for c in range(tn // LANES):
            y_sc[c] = y[:, LANES * c:LANES * (c + 1)]
        for ref, dil in zip(class_refs, class_dils):
            for r in range(dil):
                for c in range(tn // LANES):
                    ref[r, :, LANES * c:LANES * (c + 1)] = (
                        y_sc[c, pl.ds(r, tm // dil, stride=dil), :].astype(ref.dtype))


def _norm_matmul(x, g, w, col_scale, *, class_dils=(), tm=1024, tn=1024):
    m, d = x.shape
    n = w.shape[1]
    tm = min(tm, m)
    out_specs = [pl.BlockSpec((tm, tn), lambda i, j: (i, j))]
    out_shape = [jax.ShapeDtypeStruct((m, n), BF16)]
    scratch = [pltpu.VMEM((tm, d), BF16)]
    for dil in class_dils:
        out_specs.append(pl.BlockSpec((dil, tm // dil, tn), lambda i, j: (0, i, j)))
        out_shape.append(jax.ShapeDtypeStruct((dil, m // dil, n), BF16))
    if class_dils:
        scratch.append(pltpu.VMEM((tn // LANES, tm, LANES), F32))
    return pl.pallas_call(
        functools.partial(_norm_matmul_kernel, class_dils=tuple(class_dils)),
        grid=(m // tm, n // tn),
        in_specs=[pl.BlockSpec((tm, d), lambda i, j: (i, 0)),
                  pl.BlockSpec((1, d), lambda i, j: (0, 0)),
                  pl.BlockSpec((d, tn), lambda i, j: (0, j)),
                  pl.BlockSpec((1, tn), lambda i, j: (0, j))],
        out_specs=out_specs,
        out_shape=out_shape,
        scratch_shapes=scratch,
        compiler_params=_params("parallel", "arbitrary"),
        name="norm_matmul",
    )(x, g.reshape(1, d), w, col_scale)


def _matmul_residual_kernel(a_ref, w_ref, r_ref, o_ref, wb_ref):
    @pl.when(pl.program_id(0) == 0)
    def _():
        wb_ref[...] = w_ref[...].astype(BF16)

    o_ref[...] = r_ref[...] + _dot(a_ref[...], wb_ref[...])


def _matmul_residual(a, w, res, *, tm=512):
    m, k = a.shape
    n = w.shape[1]
    tm = min(tm, m)
    return pl.pallas_call(
        _matmul_residual_kernel,
        grid=(m // tm,),
        in_specs=[pl.BlockSpec((tm, k), lambda i: (i, 0)),
                  pl.BlockSpec((k, n), lambda i: (0, 0), pipeline_mode=pl.Buffered(1)),
                  pl.BlockSpec((tm, n), lambda i: (i, 0))],
        out_specs=pl.BlockSpec((tm, n), lambda i: (i, 0)),
        out_shape=jax.ShapeDtypeStruct((m, n), F32),
        scratch_shapes=[pltpu.VMEM((k, n), BF16)],
        compiler_params=_params("arbitrary"),
        name="matmul_residual",
    )(a, w, res)


def _mlp_kernel(x_ref, g_ref, w1_ref, w2_ref, gf_ref, o_ref, hn_ref, *, final_norm):
    c = pl.program_id(1)

    @pl.when(c == 0)
    def _():
        x = x_ref[...]
        hn_ref[...] = _rms(x, g_ref[...]).astype(BF16)
        o_ref[...] = x

    u = jnp.maximum(_dot(hn_ref[...], w1_ref[...]), 0.0)
    o_ref[...] += _dot((u * u).astype(BF16), w2_ref[...])

    if final_norm:
        @pl.when(c == pl.num_programs(1) - 1)
        def _():
            o_ref[...] = _rms(o_ref[...], gf_ref[...])


def _mlp(x, g, w1, w2, gf, *, final_norm, tm=1024, tf=512):
    m, d = x.shape
    f = w1.shape[1]
    tm = min(tm, m)
    return pl.pallas_call(
        functools.partial(_mlp_kernel, final_norm=final_norm),
        grid=(m // tm, f // tf),
        in_specs=[pl.BlockSpec((tm, d), lambda i, c: (i, 0)),
                  pl.BlockSpec((1, d), lambda i, c: (0, 0)),
                  pl.BlockSpec((d, tf), lambda i, c: (0, c)),
                  pl.BlockSpec((tf, d), lambda i, c: (c, 0)),
                  pl.BlockSpec((1, d), lambda i, c: (0, 0))],
        out_specs=pl.BlockSpec((tm, d), lambda i, c: (i, 0)),
        out_shape=jax.ShapeDtypeStruct((m, d), F32),
        scratch_shapes=[pltpu.VMEM((tm, d), BF16)],
        compiler_params=_params("parallel", "arbitrary"),
        name="mlp",
    )(x, g.reshape(1, d), w1, w2, gf.reshape(1, d))


def _dilated_bias_kernel(tab_ref, o_ref):
    h = pl.program_id(0)
    row = lax.broadcasted_iota(jnp.int32, (DIL_SUB, DIL_BAND), 0)
    col = lax.broadcasted_iota(jnp.int32, (DIL_SUB, DIL_BAND), 1)
    steps = col - row - HALF_STEPS
    in_band = jnp.abs(steps) <= HALF_STEPS
    for bi, (_, dil) in enumerate(DILATED_CONFIGS):
        reach = HALF_STEPS * dil
        rel = jnp.clip(steps, -HALF_STEPS, HALF_STEPS) * dil
        bias = _bias_chain(rel, -reach, reach, lambda b: tab_ref[b, h] * LOG2_E)
        o_ref[bi, 0] = jnp.where(in_band, bias, NEG_INF)


def _dilated_bias(table):
    nb = len(DILATED_CONFIGS)
    return pl.pallas_call(
        _dilated_bias_kernel,
        grid=(A_HEADS,),
        in_specs=[pl.BlockSpec(memory_space=pltpu.SMEM)],
        out_specs=pl.BlockSpec((nb, 1, DIL_SUB, DIL_BAND), lambda h: (0, h, 0, 0)),
        out_shape=jax.ShapeDtypeStruct((nb, A_HEADS, DIL_SUB, DIL_BAND), F32),
        compiler_params=_params("arbitrary"),
        name="dilated_bias",
    )(table)


def _dilated_branch_kernel(q_ref, kp_ref, kc_ref, kn_ref, vp_ref, vc_ref, vn_ref, b_ref,
                           o_ref, lse_ref, *, tq):
    tb = pl.program_id(2)
    last_tb = pl.num_programs(2) - 1
    nsub = tq // DIL_SUB
    col = lax.broadcasted_iota(jnp.int32, (DIL_SUB, DIL_BAND), 1)
    lane = lax.broadcasted_iota(jnp.int32, (DIL_SUB, LANES), 1)
    lanes_per_head = LANES // DIL_HG

    def band(prev_ref, cur_ref, next_ref, i, hs):
        lo = DIL_SUB * i - HALF_STEPS
        hi = lo + DIL_BAND
        parts = []
        if lo < 0:
            parts.append(prev_ref[:, hs])
        parts.append(cur_ref[max(lo, 0):min(hi, tq), hs])
        if hi > tq:
            parts.append(next_ref[:, hs])
        return parts[0] if len(parts) == 1 else jnp.concatenate(parts, axis=0)

    for i in range(nsub):
        lse_blk = None
        for j in range(DIL_HG):
            hs = slice(HEAD_DIM * j, HEAD_DIM * (j + 1))
            q = q_ref[DIL_SUB * i:DIL_SUB * (i + 1), hs]
            kb = band(kp_ref, kc_ref, kn_ref, i, hs)
            vb = band(vp_ref, vc_ref, vn_ref, i, hs)
            s = _dot_nt(q, kb) + b_ref[j]
            if i == 0:
                s = jnp.where((tb > 0) | (col >= HALF_STEPS), s, NEG_INF)
            if i == nsub - 1:
                s = jnp.where((tb < last_tb) | (col < DIL_BAND - HALF_STEPS), s, NEG_INF)
            m = jnp.max(s, axis=-1, keepdims=True)
            p = jnp.exp2(s - m)
            den = jnp.sum(p, axis=-1, keepdims=True)
            o = _dot(p.astype(BF16), vb) / den
            o_ref[DIL_SUB * i:DIL_SUB * (i + 1), hs] = o.astype(o_ref.dtype)
            lse = m + jnp.log2(den)
            lse_blk = lse if lse_blk is None else jnp.where(lane >= lanes_per_head * j, lse, lse_blk)
        lse_ref[DIL_SUB * i:DIL_SUB * (i + 1), :] = jnp.broadcast_to(lse_blk, (DIL_SUB, LANES))


def _dilated_branch(proj_c, bias_b, *, tq=DIL_TQ):
    dil, l, w = proj_c.shape
    tq = min(tq, l)
    ngroups = A_HEADS // DIL_HG
    gw = DIL_HG * HEAD_DIM
    qoff, koff, voff = 0, A_WIDTH // gw, 2 * A_WIDTH // gw
    halo_per_tq = tq // HALF_STEPS
    n_halo = l // HALF_STEPS

    def cur(off):
        return pl.BlockSpec((None, tq, gw), lambda r, g, t: (r, t, off + g))

    def prev(off):
        return pl.BlockSpec((None, HALF_STEPS, gw),
                            lambda r, g, t: (r, jnp.maximum(t * halo_per_tq - 1, 0), off + g))

    def nxt(off):
        return pl.BlockSpec((None, HALF_STEPS, gw),
                            lambda r, g, t: (r, jnp.minimum((t + 1) * halo_per_tq, n_halo - 1), off + g))

    return pl.pallas_call(
        functools.partial(_dilated_branch_kernel, tq=tq),
        grid=(dil, ngroups, l // tq),
        in_specs=[cur(qoff), prev(koff), cur(koff), nxt(koff), prev(voff), cur(voff), nxt(voff),
                  pl.BlockSpec((DIL_HG, DIL_SUB, DIL_BAND), lambda r, g, t: (g, 0, 0))],
        out_specs=[pl.BlockSpec((None, tq, gw), lambda r, g, t: (r, t, g)),
                   pl.BlockSpec((None, tq, LANES), lambda r, g, t: (r, t, g))],
        out_shape=[jax.ShapeDtypeStruct((dil, l, A_WIDTH), BF16),
                   jax.ShapeDtypeStruct((dil, l, ngroups * LANES), F32)],
        compiler_params=_params("parallel", "parallel", "arbitrary"),
        name=f"dilated_branch_d{dil}",
    )(*([proj_c] * 7), bias_b)


def _dilated_merge_kernel(*refs, dils):
    nb = len(dils)
    o_refs, l_refs = refs[:nb], refs[nb:2 * nb]
    fb_ref, out_ref, l_sc, o_sc = refs[2 * nb:]
    g = pl.program_id(1)
    ngroups = A_HEADS // DIL_HG
    lanes_per_head = LANES // DIL_HG
    tm = out_ref.shape[0]

    @pl.when(g < ngroups)
    def _():
        for b, dil in enumerate(dils):
            for r in range(dil):
                rows = pl.ds(r, tm // dil, stride=dil)
                l_sc[b, rows, :] = l_refs[b][r]
                for j in range(DIL_HG):
                    o_sc[b, j, rows, :] = o_refs[b][r, :, HEAD_DIM * j:HEAD_DIM * (j + 1)].astype(F32)
        ls = [l_sc[b] for b in range(nb)]
        mx = functools.reduce(jnp.maximum, ls)
        es = [jnp.exp2(l - mx) for l in ls]
        inv = 1.0 / functools.reduce(jnp.add, es)
        for j in range(DIL_HG):
            lc = slice(lanes_per_head * j, lanes_per_head * j + 1)
            acc = functools.reduce(jnp.add, [(es[b] * inv)[:, lc] * o_sc[b, j] for b in range(nb)])
            out_ref[:, HEAD_DIM * j:HEAD_DIM * (j + 1)] = acc.astype(out_ref.dtype)

    @pl.when(g == ngroups)
    def _():
        out_ref[...] = fb_ref[...]


def _dilated_merge(os_, lses, fb, *, tm=1024):
    s = fb.shape[0]
    tm = min(tm, s)
    ngroups = A_HEADS // DIL_HG
    gw = DIL_HG * HEAD_DIM
    assert fb.shape[1] == gw and HEAD_DIM == LANES
    dils = tuple(o.shape[0] for o in os_)
    clamp = lambda i, g: (0, i, jnp.minimum(g, ngroups - 1))
    return pl.pallas_call(
        functools.partial(_dilated_merge_kernel, dils=dils),
        grid=(s // tm, ngroups + 1),
        in_specs=[pl.BlockSpec((d, tm // d, gw), clamp) for d in dils]
                 + [pl.BlockSpec((d, tm // d, LANES), clamp) for d in dils]
                 + [pl.BlockSpec((tm, gw), lambda i, g: (i, 0))],
        out_specs=pl.BlockSpec((tm, gw), lambda i, g: (i, g)),
        out_shape=jax.ShapeDtypeStruct((s, A_WIDTH + B_WIDTH), BF16),
        scratch_shapes=[pltpu.VMEM((len(dils), tm, LANES), F32),
                        pltpu.VMEM((len(dils), DIL_HG, tm, LANES), F32)],
        compiler_params=_params("parallel", "arbitrary"),
        name="dilated_merge",
    )(*os_, *lses, fb)


def _dft_cos_sin(n):
    idx = np.arange(n, dtype=np.int64)
    ang = 2.0 * np.pi * ((idx[:, None] * idx[None, :]) % n) / n
    return np.cos(ang), np.sin(ang)


def _fourier_weights_kernel(cs_ref, w_ref, o_ref, *, norm):
    w = w_ref[0]
    ab = jnp.dot(cs_ref[...], w, preferred_element_type=F32, precision=lax.Precision.HIGHEST) * norm
    o_ref[0] = jnp.concatenate([ab[:HEAD_DIM], ab[HEAD_DIM:]], axis=1).astype(o_ref.dtype)


def _fourier_weights(w_f, seq):
    c, s = _dft_cos_sin(HEAD_DIM)
    cs = jnp.asarray(np.concatenate([c, s], axis=0), F32)
    norm = 1.0 / math.sqrt(seq * HEAD_DIM)
    return pl.pallas_call(
        functools.partial(_fourier_weights_kernel, norm=norm),
        grid=(B_GROUPS,),
        in_specs=[pl.BlockSpec((2 * HEAD_DIM, HEAD_DIM), lambda g: (0, 0)),
                  pl.BlockSpec((1, HEAD_DIM, HEAD_DIM), lambda g: (g, 0, 0))],
        out_specs=pl.BlockSpec((1, HEAD_DIM, 2 * HEAD_DIM), lambda g: (g, 0, 0)),
        out_shape=jax.ShapeDtypeStruct((B_GROUPS, HEAD_DIM, 2 * HEAD_DIM), BF16),
        compiler_params=_params("arbitrary"),
        name="fourier_weights",
    )(cs, w_f)


def _fourier_channel_kernel(u_ref, ab_ref, y_ref, z_ref):
    for g in range(B_GROUPS):
        hs = slice(HEAD_DIM * g, HEAD_DIM * (g + 1))
        yz = _dot(u_ref[:, hs], ab_ref[g])
        y_ref[:, hs] = yz[:, :HEAD_DIM].astype(y_ref.dtype)
        z_ref[:, hs] = yz[:, HEAD_DIM:].astype(z_ref.dtype)


def _fourier_channel(proj, ab, *, tm=1024):
    s, w = proj.shape
    ublock = (w - B_WIDTH) // B_WIDTH
    assert ublock * B_WIDTH == w - B_WIDTH
    spec = pl.BlockSpec((tm, B_WIDTH), lambda i: (i, 0))
    return pl.pallas_call(
        _fourier_channel_kernel,
        grid=(s // tm,),
        in_specs=[pl.BlockSpec((tm, B_WIDTH), lambda i: (i, ublock)),
                  pl.BlockSpec((B_GROUPS, HEAD_DIM, 2 * HEAD_DIM), lambda i: (0, 0, 0))],
        out_specs=[spec, spec],
        out_shape=[jax.ShapeDtypeStruct((s, B_WIDTH), BF16)] * 2,
        compiler_params=_params("parallel"),
        name="fourier_channel",
    )(proj, ab)


def _fourier_stage1_kernel(m1_ref, tc_ref, ts_ref, y_ref, z_ref, tre_ref, tim_ref, *, n1, n2_per_step):
    yz = jnp.concatenate([y_ref[...], z_ref[...]], axis=0)
    ab = _dot(m1_ref[...], yz)
    a, b = ab[:n1], ab[n1:]
    ch = B_WIDTH
    for q in range(n2_per_step):
        cs = slice(ch * q, ch * (q + 1))
        c = tc_ref[0, :, q:q + 1]
        s = ts_ref[0, :, q:q + 1]
        aq, bq = a[:, cs], b[:, cs]
        tre_ref[:, cs] = (aq * c + bq * s).astype(tre_ref.dtype)
        tim_ref[:, cs] = (bq * c - aq * s).astype(tim_ref.dtype)


def _fourier_stage2_kernel(m2_ref, tre_ref, tim_ref, o_ref, *, n2, k1_per_step):
    ch = B_WIDTH
    for q in range(k1_per_step):
        rs = slice(n2 * q, n2 * (q + 1))
        t = jnp.concatenate([tre_ref[rs, :], tim_ref[rs, :]], axis=0)
        o_ref[:, ch * q:ch * (q + 1)] = _dot(m2_ref[...], t).astype(o_ref.dtype)


def _fourier_position(y, z, *, n2=FFT_N2, n2_per_step=16, k1_per_step=4):
    seq, ch = y.shape
    n1 = seq // n2
    n2_per_step = min(n2_per_step, n2)
    k1_per_step = min(k1_per_step, n1)
    c1, s1 = _dft_cos_sin(n1)
    m1 = jnp.asarray(np.block([[c1, -s1], [-s1, -c1]]), BF16)
    c2, s2 = _dft_cos_sin(n2)
    m2 = jnp.asarray(np.concatenate([c2, s2], axis=1), BF16)
    k1 = np.arange(n1, dtype=np.int64)[:, None]
    nn2 = np.arange(n2, dtype=np.int64)[None, :]
    ang = 2.0 * np.pi * ((k1 * nn2) % seq) / seq
    steps = n2 // n2_per_step
    tc = jnp.asarray(np.cos(ang).reshape(n1, steps, n2_per_step).transpose(1, 0, 2), F32)
    ts = jnp.asarray(np.sin(ang).reshape(n1, steps, n2_per_step).transpose(1, 0, 2), F32)

    cols = n2_per_step * ch
    dspec = pl.BlockSpec((n1, cols), lambda t: (0, t))
    tspec = pl.BlockSpec((1, n1, n2_per_step), lambda t: (t, 0, 0))
    tre, tim = pl.pallas_call(
        functools.partial(_fourier_stage1_kernel, n1=n1, n2_per_step=n2_per_step),
        grid=(steps,),
        in_specs=[pl.BlockSpec((2 * n1, 2 * n1), lambda t: (0, 0)), tspec, tspec, dspec, dspec],
        out_specs=[dspec, dspec],
        out_shape=[jax.ShapeDtypeStruct((n1, n2 * ch), BF16)] * 2,
        compiler_params=_params("parallel"),
        name="fourier_stage1",
    )(m1, tc, ts, y.reshape(n1, n2 * ch), z.reshape(n1, n2 * ch))

    tblock = pl.BlockSpec((k1_per_step * n2, ch), lambda t: (t, 0))
    out = pl.pallas_call(
        functools.partial(_fourier_stage2_kernel, n2=n2, k1_per_step=k1_per_step),
        grid=(n1 // k1_per_step,),
        in_specs=[pl.BlockSpec((n2, 2 * n2), lambda t: (0, 0)), tblock, tblock],
        out_specs=pl.BlockSpec((n2, k1_per_step * ch), lambda t: (0, t)),
        out_shape=jax.ShapeDtypeStruct((n2, n1 * ch), BF16),
        compiler_params=_params("parallel"),
        name="fourier_stage2",
    )(m2, tre.reshape(seq, ch), tim.reshape(seq, ch))
    return out.reshape(seq, ch)


def _diff_bias_kernel(tab_ref, o_ref, *, t):
    h = pl.program_id(0)
    row = lax.broadcasted_iota(jnp.int32, (t, t), 0)
    col = lax.broadcasted_iota(jnp.int32, (t, t), 1)
    base = row - col
    for r in range(2 * DIFF_R + 1):
        d = (r - DIFF_R) * t
        o_ref[0, r] = _bias_chain(base + d, d - (t - 1), d + (t - 1),
                                  lambda b: tab_ref[b, A_HEADS + h] * LOG2_E)


def _diff_bias(table, t):
    assert DIFF_R * t - (t - 1) >= _THR[_HALF_BUCKETS - 1]
    nt = 2 * DIFF_R + 1
    return pl.pallas_call(
        functools.partial(_diff_bias_kernel, t=t),
        grid=(C_HEADS,),
        in_specs=[pl.BlockSpec(memory_space=pltpu.SMEM)],
        out_specs=pl.BlockSpec((1, nt, t, t), lambda h: (h, 0, 0, 0)),
        out_shape=jax.ShapeDtypeStruct((C_HEADS, nt, t, t), F32),
        compiler_params=_params("arbitrary"),
        name="diff_bias",
    )(table)


def _diff_attn_kernel(q_ref, k_ref, v_ref, b_ref, lam_ref, g_ref, o_ref,
                      vt_sc, s0_sc, s1_sc, p0_sc, p1_sc, alpha0_sc, alpha1_sc, m_sc, acc_sc,
                      *, t, tk, lambda_init):
    s_sc, p_sc, alpha_sc = (s0_sc, s1_sc), (p0_sc, p1_sc), (alpha0_sc, alpha1_sc)
    qg = pl.program_id(1)
    nkv = k_ref.shape[0] // tk
    ntile = q_ref.shape[0] // t
    total = ntile * nkv
    per_tile = t // tk

    @pl.when(qg == 0)
    def _():
        pad = lax.broadcasted_iota(jnp.int32, (DIFF_VT_ROWS - C_V_DIM, tk), 0)
        ones_row = jnp.where(pad == 0, 1.0, 0.0).astype(BF16)
        for j in range(nkv):
            vt_sc[j, :C_V_DIM, :] = v_ref[tk * j:tk * (j + 1), :].astype(F32).T.astype(BF16)
            vt_sc[j, C_V_DIM:, :] = ones_row

    def split(n):
        return n // nkv, n % nkv

    def key_rows(j):
        return pl.ds(pl.multiple_of(j * tk, tk), tk)

    def query_rows(tile):
        return pl.ds(pl.multiple_of(tile * t, t), t)

    def logits(n, buf):
        tile, j = split(n)
        q_tile = qg * ntile + tile
        for c in range(2):
            hs = slice(HEAD_DIM * c, HEAD_DIM * (c + 1))
            q = q_ref[query_rows(tile), hs]
            if tk <= t:
                bias_tile = jnp.clip(j // per_tile - q_tile, -DIFF_R, DIFF_R) + DIFF_R
                bias = b_ref[0, bias_tile, pl.ds(pl.multiple_of((j % per_tile) * tk, tk), tk), :]
                s_sc[buf][c] = _dot_nt(k_ref[key_rows(j), hs], q) + bias
            else:
                for r in range(tk // t):
                    bias_tile = jnp.clip(j * (tk // t) + r - q_tile, -DIFF_R, DIFF_R) + DIFF_R
                    k_rows = pl.ds(pl.multiple_of(j * tk + r * t, t), t)
                    s_sc[buf][c, t * r:t * (r + 1), :] = _dot_nt(k_ref[k_rows, hs], q) + b_ref[0, bias_tile]

    def softmax(n, buf):
        _, j = split(n)
        for c in range(2):
            m_cur = jnp.max(s_sc[buf][c], axis=0, keepdims=True)
            m_prev = jnp.where(j == 0, NEG_INF, m_sc[c])
            m_next = jnp.maximum(m_prev, m_cur)
            alpha_sc[buf][c] = jnp.exp2(m_prev - m_next)
            m_sc[c] = m_next
            p_sc[buf][c] = jnp.exp2(s_sc[buf][c] - m_next).astype(BF16)

    def values(n, buf):
        _, j = split(n)
        for c in range(2):
            acc_sc[c] = acc_sc[c] * alpha_sc[buf][c] + _dot(vt_sc[j], p_sc[buf][c])

    def finalize(tile):
        lp = lam_ref[...]
        lam = (jnp.exp(jnp.sum(lp[0:1] * lp[1:2], axis=-1, keepdims=True))
               - jnp.exp(jnp.sum(lp[2:3] * lp[3:4], axis=-1, keepdims=True)) + lambda_init)
        num0, den0 = acc_sc[0, :C_V_DIM, :], acc_sc[0, C_V_DIM:C_V_DIM + 1, :]
        num1, den1 = acc_sc[1, :C_V_DIM, :], acc_sc[1, C_V_DIM:C_V_DIM + 1, :]
        o = num0 * (1.0 / den0) - lam * (num1 * (1.0 / den1))
        ms = jnp.mean(o * o, axis=0, keepdims=True)
        o = o * lax.rsqrt(ms + NORM_EPS) * (g_ref[...] * (1.0 - lambda_init))
        o_ref[query_rows(tile), :] = o.T.astype(o_ref.dtype)

    m_sc[...] = jnp.full(m_sc.shape, NEG_INF, F32)
    acc_sc[...] = jnp.zeros(acc_sc.shape, F32)

    logits(0, 0)
    logits(1, 1)
    softmax(0, 0)

    def body(i, carry):
        n = 2 * i + 1
        logits(n + 1, 0)
        softmax(n, 1)
        values(n - 1, 0)
        logits(n + 2, 1)
        softmax(n + 1, 0)
        values(n, 1)

        @pl.when((n + 1) % nkv == 0)
        def _():
            finalize(n // nkv)

        return carry

    lax.fori_loop(0, total // 2 - 1, body, 0)
    softmax(total - 1, 1)
    values(total - 2, 0)
    values(total - 1, 1)
    finalize(ntile - 1)


def _diff_attn(proj, bias, lam_params, subln_g, lambda_init, *, t, tk=DIFF_TK, tiles_per_step=DIFF_TILES_PER_STEP):
    s = proj.shape[0]
    nt = 2 * DIFF_R + 1
    tk = min(tk, s // 4)
    nkv = s // tk
    tiles_per_step = min(tiles_per_step, s // t)
    tq = tiles_per_step * t
    assert nkv * tk == s and nkv % 2 == 0 and nkv >= 4 and (t % tk == 0 or tk % t == 0) and s % tq == 0
    return pl.pallas_call(
        functools.partial(_diff_attn_kernel, t=t, tk=tk, lambda_init=lambda_init),
        grid=(C_HEADS, s // tq),
        in_specs=[pl.BlockSpec((tq, C_V_DIM), lambda h, i: (i, h)),
                  pl.BlockSpec((s, C_V_DIM), lambda h, i: (0, C_HEADS + h)),
                  pl.BlockSpec((s, C_V_DIM), lambda h, i: (0, 2 * C_HEADS + h)),
                  pl.BlockSpec((1, nt, t, t), lambda h, i: (h, 0, 0, 0)),
                  pl.BlockSpec((4, HEAD_DIM), lambda h, i: (0, 0)),
                  pl.BlockSpec((C_V_DIM, 1), lambda h, i: (0, 0))],
        out_specs=pl.BlockSpec((tq, C_V_DIM), lambda h, i: (i, h)),
        out_shape=jax.ShapeDtypeStruct((s, C_HEADS * C_V_DIM), BF16),
        scratch_shapes=[pltpu.VMEM((nkv, DIFF_VT_ROWS, tk), BF16),
                        pltpu.VMEM((2, tk, t), F32), pltpu.VMEM((2, tk, t), F32),
                        pltpu.VMEM((2, tk, t), BF16), pltpu.VMEM((2, tk, t), BF16),
                        pltpu.VMEM((2, 1, t), F32), pltpu.VMEM((2, 1, t), F32), pltpu.VMEM((2, 1, t), F32),
                        pltpu.VMEM((2, DIFF_VT_ROWS, t), F32)],
        compiler_params=_params("arbitrary", "arbitrary"),
        name="diff_attn",
    )(proj, proj, proj, bias, lam_params, subln_g.reshape(C_V_DIM, 1))


def kernel(x, norm_mix_g, norm_ffn_g, norm_final_g, rel_bias_table, w_in_even, w_fnet, w_out_even,
           w_qkv_odd, lambda_q1, lambda_k1, lambda_q2, lambda_k2, subln_g, w_out_odd, w_ff1, w_ff2):
    batch, seq, d_model = x.shape
    depth = norm_mix_g.shape[0]
    table = rel_bias_table.astype(F32)
    dil_bias = _dilated_bias(table)
    diff_t = min(DIFF_T, seq)
    diff_bias = _diff_bias(table, diff_t)
    c_qk_width = C_HEADS * 2 * HEAD_DIM
    odd_scale = jnp.concatenate([jnp.full((1, c_qk_width), LOG2_E / math.sqrt(HEAD_DIM), F32),
                                 jnp.ones((1, w_qkv_odd.shape[2] - c_qk_width), F32)], axis=1)
    even_scale = jnp.concatenate([jnp.full((1, A_WIDTH), LOG2_E / math.sqrt(HEAD_DIM), F32),
                                  jnp.ones((1, w_in_even.shape[2] - A_WIDTH), F32)], axis=1)

    outs = []
    for bidx in range(batch):
        xs = x[bidx]
        for i in range(depth):
            j = i // 2
            if i % 2 == 0:
                class_dils = tuple(dil for _, dil in DILATED_CONFIGS if dil > 1)
                proj, *by_class = _norm_matmul(xs, norm_mix_g[i], w_in_even[j].astype(BF16), even_scale,
                                               class_dils=class_dils)
                by_class = dict(zip(class_dils, by_class))
                by_class[1] = proj.reshape(1, *proj.shape)
                branches = [_dilated_branch(by_class[dil], dil_bias[bi])
                            for bi, (_, dil) in enumerate(DILATED_CONFIGS)]
                y, z = _fourier_channel(proj, _fourier_weights(w_fnet[j], seq))
                fb = _fourier_position(y, z)
                mixed = _dilated_merge([b[0] for b in branches], [b[1] for b in branches], fb)
                xs = _matmul_residual(mixed, w_out_even[j].astype(F32), xs)
            else:
                lambda_init = 0.8 - 0.6 * math.exp(-0.3 * i)
                proj, = _norm_matmul(xs, norm_mix_g[i], w_qkv_odd[j].astype(BF16), odd_scale)
                lam_params = jnp.stack([lambda_q1[j], lambda_k1[j], lambda_q2[j], lambda_k2[j]]).astype(F32)
                attn = _diff_attn(proj, diff_bias, lam_params, subln_g[j], lambda_init, t=diff_t)
                xs = _matmul_residual(attn, w_out_odd[j].astype(F32), xs)
            xs = _mlp(xs, norm_ffn_g[i], w_ff1[i].astype(BF16), w_ff2[i].astype(BF16), norm_final_g,
                      final_norm=(i == depth - 1))
        outs.append(xs)
    return jnp.stack(outs, axis=0)
```

```python
import functools
import math

import numpy as np
import jax
import jax.numpy as jnp
from jax import lax
from jax.experimental import pallas as pl
from jax.experimental.pallas import tpu as pltpu

F32 = jnp.float32
BF16 = jnp.bfloat16

HEAD_DIM = 128
A_HEADS = 12
B_GROUPS = 4
C_HEADS = 8
DILATED_CONFIGS = ((128, 1), (512, 4), (2048, 16))
NUM_BUCKETS = 32
MAX_DISTANCE = 1024
NORM_EPS = 1e-6
NEG_INF = -1e30
LOG2_E = math.log2(math.e)

A_WIDTH = A_HEADS * HEAD_DIM
B_WIDTH = B_GROUPS * HEAD_DIM
C_V_DIM = 2 * HEAD_DIM
HALF_STEPS = 64
assert all(w // (2 * d) == HALF_STEPS for w, d in DILATED_CONFIGS)

LANES = 128
V7X_VMEM_LIMIT_BYTES = 56 * 1024 * 1024

DIL_TQ = 1024
DIL_SUB = 128
DIL_BAND = DIL_SUB + 2 * HALF_STEPS
DIL_HG = 4
DIFF_T = 512
DIFF_TK = 512
DIFF_TILES_PER_STEP = 8
DIFF_R = 3
BF16_SUBLANES = 16
DIFF_VT_ROWS = C_V_DIM + BF16_SUBLANES
FFT_N2 = 128


def _params(*sem, flags=None):
    return pltpu.CompilerParams(dimension_semantics=sem, vmem_limit_bytes=V7X_VMEM_LIMIT_BYTES, flags=flags)


def _magnitude_thresholds():
    half = NUM_BUCKETS // 2
    max_exact = half // 2
    n = np.arange(1, 4 * MAX_DISTANCE, dtype=np.int64)
    large = max_exact + (np.log(n / max_exact) / math.log(MAX_DISTANCE / max_exact)
                         * (half - max_exact)).astype(np.int64)
    large = np.minimum(large, half - 1)
    bucket = np.where(n < max_exact, n, large)
    assert np.all(np.diff(bucket) >= 0)
    thr = {b: int(n[np.argmax(bucket >= b)]) for b in range(1, half)}
    return thr


_THR = _magnitude_thresholds()
_HALF_BUCKETS = NUM_BUCKETS // 2


def _mag_bucket(n):
    return sum(1 for b in range(1, _HALF_BUCKETS) if n >= _THR[b])


def _bias_chain(rel, lo, hi, tab):
    n = jnp.abs(rel)

    def side(base, nlo, nhi):
        bmin, bmax = _mag_bucket(nlo), _mag_bucket(nhi)
        val = jnp.full(rel.shape, tab(base + bmax), F32)
        for b in range(bmax - 1, bmin - 1, -1):
            val = jnp.where(n < _THR[b + 1], tab(base + b), val)
        return val

    if lo > 0:
        return side(_HALF_BUCKETS, lo, hi)
    if hi <= 0:
        return side(0, -hi, -lo)
    return jnp.where(rel > 0, side(_HALF_BUCKETS, 1, hi), side(0, 0, -lo))


def _rms(xf, g):
    ms = jnp.mean(xf * xf, axis=-1, keepdims=True)
    return xf * lax.rsqrt(ms + NORM_EPS) * g


def _dot(a, b):
    return jnp.dot(a, b, preferred_element_type=F32)


def _dot_nt(a, b):
    return lax.dot_general(a, b, (((1,), (1,)), ((), ())), preferred_element_type=F32)


def _norm_matmul_kernel(x_ref, g_ref, w_ref, cs_ref, o_ref, *rest, class_dils):
    class_refs = rest[:len(class_dils)]
    hn_ref = rest[len(class_dils)]

    @pl.when(pl.program_id(1) == 0)
    def _():
        hn_ref[...] = _rms(x_ref[...], g_ref[...]).astype(BF16)

    y = _dot(hn_ref[...], w_ref[...]) * cs_ref[...]
    o_ref[...] = y.astype(o_ref.dtype)

    if class_dils:
        y_sc = rest[len(class_dils) + 1]
        tm, tn = y.shape
        for c in range(tn // LANES):
            y_sc[c] = y[:, LANES * c:LANES * (c + 1)]
        for ref, dil in zip(class_refs, class_dils):
            for r in range(dil):
                for c in range(tn // LANES):
                    ref[r, :, LANES * c:LANES * (c + 1)] = (
                        y_sc[c, pl.ds(r, tm // dil, stride=dil), :].astype(ref.dtype))


def _norm_matmul(x, g, w, layer, col_scale, *, class_dils=(), tm=1024, tn=1024):
    m, d = x.shape
    n = w.shape[2]
    tm = min(tm, m)
    out_specs = [pl.BlockSpec((tm, tn), lambda i, j: (i, j))]
    out_shape = [jax.ShapeDtypeStruct((m, n), BF16)]
    scratch = [pltpu.VMEM((tm, d), BF16)]
    for dil in class_dils:
        out_specs.append(pl.BlockSpec((dil, tm // dil, tn), lambda i, j: (0, i, j)))
        out_shape.append(jax.ShapeDtypeStruct((dil, m // dil, n), BF16))
    if class_dils:
        scratch.append(pltpu.VMEM((tn // LANES, tm, LANES), F32))
    return pl.pallas_call(
        functools.partial(_norm_matmul_kernel, class_dils=tuple(class_dils)),
        grid=(m // tm, n // tn),
        in_specs=[pl.BlockSpec((tm, d), lambda i, j: (i, 0)),
                  pl.BlockSpec((1, d), lambda i, j: (0, 0)),
                  pl.BlockSpec((None, d, tn), lambda i, j: (layer, 0, j)),
                  pl.BlockSpec((1, tn), lambda i, j: (0, j))],
        out_specs=out_specs,
        out_shape=out_shape,
        scratch_shapes=scratch,
        compiler_params=_params("parallel", "arbitrary"),
        name="norm_matmul",
    )(x, g.reshape(1, d), w, col_scale)


def _matmul_residual_kernel(a_ref, w_ref, r_ref, o_ref, wb_ref):
    @pl.when(pl.program_id(0) == 0)
    def _():
        wb_ref[...] = w_ref[...].astype(BF16)

    o_ref[...] = r_ref[...] + _dot(a_ref[...], wb_ref[...])


def _matmul_residual(a, w, layer, res, *, tm=512):
    m, k = a.shape
    n = w.shape[2]
    tm = min(tm, m)
    return pl.pallas_call(
        _matmul_residual_kernel,
        grid=(m // tm,),
        in_specs=[pl.BlockSpec((tm, k), lambda i: (i, 0)),
                  pl.BlockSpec((None, k, n), lambda i: (layer, 0, 0), pipeline_mode=pl.Buffered(1)),
                  pl.BlockSpec((tm, n), lambda i: (i, 0))],
        out_specs=pl.BlockSpec((tm, n), lambda i: (i, 0)),
        out_shape=jax.ShapeDtypeStruct((m, n), F32),
        scratch_shapes=[pltpu.VMEM((k, n), BF16)],
        compiler_params=_params("arbitrary"),
        name="matmul_residual",
    )(a, w, res)


def _mlp_kernel(x_ref, g_ref, w1_ref, w2_ref, gf_ref, o_ref, hn_ref, *, final_norm):
    c = pl.program_id(1)

    @pl.when(c == 0)
    def _():
        x = x_ref[...]
        hn_ref[...] = _rms(x, g_ref[...]).astype(BF16)
        o_ref[...] = x

    u = jnp.maximum(_dot(hn_ref[...], w1_ref[...]), 0.0)
    o_ref[...] += _dot((u * u).astype(BF16), w2_ref[...])

    if final_norm:
        @pl.when(c == pl.num_programs(1) - 1)
        def _():
            o_ref[...] = _rms(o_ref[...], gf_ref[...])


def _mlp(x, g, w1, w2, layer, gf, *, final_norm, tm=1024, tf=512):
    m, d = x.shape
    f = w1.shape[2]
    tm = min(tm, m)
    return pl.pallas_call(
        functools.partial(_mlp_kernel, final_norm=final_norm),
        grid=(m // tm, f // tf),
        in_specs=[pl.BlockSpec((tm, d), lambda i, c: (i, 0)),
                  pl.BlockSpec((1, d), lambda i, c: (0, 0)),
                  pl.BlockSpec((None, d, tf), lambda i, c: (layer, 0, c)),
                  pl.BlockSpec((None, tf, d), lambda i, c: (layer, c, 0)),
                  pl.BlockSpec((1, d), lambda i, c: (0, 0))],
        out_specs=pl.BlockSpec((tm, d), lambda i, c: (i, 0)),
        out_shape=jax.ShapeDtypeStruct((m, d), F32),
        scratch_shapes=[pltpu.VMEM((tm, d), BF16)],
        compiler_params=_params("parallel", "arbitrary"),
        name="mlp",
    )(x, g.reshape(1, d), w1, w2, gf.reshape(1, d))


def _dilated_bias_kernel(tab_ref, o_ref):
    h = pl.program_id(0)
    row = lax.broadcasted_iota(jnp.int32, (DIL_SUB, DIL_BAND), 0)
    col = lax.broadcasted_iota(jnp.int32, (DIL_SUB, DIL_BAND), 1)
    steps = col - row - HALF_STEPS
    in_band = jnp.abs(steps) <= HALF_STEPS
    for bi, (_, dil) in enumerate(DILATED_CONFIGS):
        reach = HALF_STEPS * dil
        rel = jnp.clip(steps, -HALF_STEPS, HALF_STEPS) * dil
        bias = _bias_chain(rel, -reach, reach, lambda b: tab_ref[b, h] * LOG2_E)
        o_ref[bi, 0] = jnp.where(in_band, bias, NEG_INF)


def _dilated_bias(table):
    nb = len(DILATED_CONFIGS)
    return pl.pallas_call(
        _dilated_bias_kernel,
        grid=(A_HEADS,),
        in_specs=[pl.BlockSpec(memory_space=pltpu.SMEM)],
        out_specs=pl.BlockSpec((nb, 1, DIL_SUB, DIL_BAND), lambda h: (0, h, 0, 0)),
        out_shape=jax.ShapeDtypeStruct((nb, A_HEADS, DIL_SUB, DIL_BAND), F32),
        compiler_params=_params("arbitrary"),
        name="dilated_bias",
    )(table)


def _dilated_branch_kernel(q_ref, kp_ref, kc_ref, kn_ref, vp_ref, vc_ref, vn_ref, b_ref,
                           o_ref, lse_ref, *, tq):
    tb = pl.program_id(2)
    last_tb = pl.num_programs(2) - 1
    nsub = tq // DIL_SUB
    col = lax.broadcasted_iota(jnp.int32, (DIL_SUB, DIL_BAND), 1)
    lane = lax.broadcasted_iota(jnp.int32, (DIL_SUB, LANES), 1)
    lanes_per_head = LANES // DIL_HG

    def band(prev_ref, cur_ref, next_ref, i, hs):
        lo = DIL_SUB * i - HALF_STEPS
        hi = lo + DIL_BAND
        parts = []
        if lo < 0:
            parts.append(prev_ref[:, hs])
        parts.append(cur_ref[max(lo, 0):min(hi, tq), hs])
        if hi > tq:
            parts.append(next_ref[:, hs])
        return parts[0] if len(parts) == 1 else jnp.concatenate(parts, axis=0)

    for i in range(nsub):
        lse_blk = None
        for j in range(DIL_HG):
            hs = slice(HEAD_DIM * j, HEAD_DIM * (j + 1))
            q = q_ref[DIL_SUB * i:DIL_SUB * (i + 1), hs]
            kb = band(kp_ref, kc_ref, kn_ref, i, hs)
            vb = band(vp_ref, vc_ref, vn_ref, i, hs)
            s = _dot_nt(q, kb) + b_ref[j]
            if i == 0:
                s = jnp.where((tb > 0) | (col >= HALF_STEPS), s, NEG_INF)
            if i == nsub - 1:
                s = jnp.where((tb < last_tb) | (col < DIL_BAND - HALF_STEPS), s, NEG_INF)
            m = jnp.max(s, axis=-1, keepdims=True)
            p = jnp.exp2(s - m)
            den = jnp.sum(p, axis=-1, keepdims=True)
            o = _dot(p.astype(BF16), vb) / den
            o_ref[DIL_SUB * i:DIL_SUB * (i + 1), hs] = o.astype(o_ref.dtype)
            lse = m + jnp.log2(den)
            lse_blk = lse if lse_blk is None else jnp.where(lane >= lanes_per_head * j, lse, lse_blk)
        lse_ref[DIL_SUB * i:DIL_SUB * (i + 1), :] = jnp.broadcast_to(lse_blk, (DIL_SUB, LANES))


def _dilated_branch(proj_c, bias_b, *, tq=DIL_TQ):
    dil, l, w = proj_c.shape
    tq = min(tq, l)
    ngroups = A_HEADS // DIL_HG
    gw = DIL_HG * HEAD_DIM
    qoff, koff, voff = 0, A_WIDTH // gw, 2 * A_WIDTH // gw
    halo_per_tq = tq // HALF_STEPS
    n_halo = l // HALF_STEPS

    def cur(off):
        return pl.BlockSpec((None, tq, gw), lambda r, g, t: (r, t, off + g))

    def prev(off):
        return pl.BlockSpec((None, HALF_STEPS, gw),
                            lambda r, g, t: (r, jnp.maximum(t * halo_per_tq - 1, 0), off + g))

    def nxt(off):
        return pl.BlockSpec((None, HALF_STEPS, gw),
                            lambda r, g, t: (r, jnp.minimum((t + 1) * halo_per_tq, n_halo - 1), off + g))

    return pl.pallas_call(
        functools.partial(_dilated_branch_kernel, tq=tq),
        grid=(dil, ngroups, l // tq),
        in_specs=[cur(qoff), prev(koff), cur(koff), nxt(koff), prev(voff), cur(voff), nxt(voff),
                  pl.BlockSpec((DIL_HG, DIL_SUB, DIL_BAND), lambda r, g, t: (g, 0, 0))],
        out_specs=[pl.BlockSpec((None, tq, gw), lambda r, g, t: (r, t, g)),
                   pl.BlockSpec((None, tq, LANES), lambda r, g, t: (r, t, g))],
        out_shape=[jax.ShapeDtypeStruct((dil, l, A_WIDTH), BF16),
                   jax.ShapeDtypeStruct((dil, l, ngroups * LANES), F32)],
        compiler_params=_params("parallel", "parallel", "arbitrary"),
        name=f"dilated_branch_d{dil}",
    )(*([proj_c] * 7), bias_b)


def _dilated_merge_kernel(*refs, dils):
    nb = len(dils)
    o_refs, l_refs = refs[:nb], refs[nb:2 * nb]
    fb_ref, out_ref, l_sc, o_sc = refs[2 * nb:]
    g = pl.program_id(1)
    ngroups = A_HEADS // DIL_HG
    lanes_per_head = LANES // DIL_HG
    tm = out_ref.shape[0]

    @pl.when(g < ngroups)
    def _():
        for b, dil in enumerate(dils):
            for r in range(dil):
                rows = pl.ds(r, tm // dil, stride=dil)
                l_sc[b, rows, :] = l_refs[b][r]
                for j in range(DIL_HG):
                    o_sc[b, j, rows, :] = o_refs[b][r, :, HEAD_DIM * j:HEAD_DIM * (j + 1)].astype(F32)
        ls = [l_sc[b] for b in range(nb)]
        mx = functools.reduce(jnp.maximum, ls)
        es = [jnp.exp2(l - mx) for l in ls]
        inv = 1.0 / functools.reduce(jnp.add, es)
        for j in range(DIL_HG):
            lc = slice(lanes_per_head * j, lanes_per_head * j + 1)
            acc = functools.reduce(jnp.add, [(es[b] * inv)[:, lc] * o_sc[b, j] for b in range(nb)])
            out_ref[:, HEAD_DIM * j:HEAD_DIM * (j + 1)] = acc.astype(out_ref.dtype)

    @pl.when(g == ngroups)
    def _():
        out_ref[...] = fb_ref[...]


def _dilated_merge(os_, lses, fb, *, tm=1024):
    s = fb.shape[0]
    tm = min(tm, s)
    ngroups = A_HEADS // DIL_HG
    gw = DIL_HG * HEAD_DIM
    assert fb.shape[1] == gw and HEAD_DIM == LANES
    dils = tuple(o.shape[0] for o in os_)
    clamp = lambda i, g: (0, i, jnp.minimum(g, ngroups - 1))
    return pl.pallas_call(
        functools.partial(_dilated_merge_kernel, dils=dils),
        grid=(s // tm, ngroups + 1),
        in_specs=[pl.BlockSpec((d, tm // d, gw), clamp) for d in dils]
                 + [pl.BlockSpec((d, tm // d, LANES), clamp) for d in dils]
                 + [pl.BlockSpec((tm, gw), lambda i, g: (i, 0))],
        out_specs=pl.BlockSpec((tm, gw), lambda i, g: (i, g)),
        out_shape=jax.ShapeDtypeStruct((s, A_WIDTH + B_WIDTH), BF16),
        scratch_shapes=[pltpu.VMEM((len(dils), tm, LANES), F32),
                        pltpu.VMEM((len(dils), DIL_HG, tm, LANES), F32)],
        compiler_params=_params("parallel", "arbitrary"),
        name="dilated_merge",
    )(*os_, *lses, fb)


def _dft_cos_sin(n):
    idx = np.arange(n, dtype=np.int64)
    ang = 2.0 * np.pi * ((idx[:, None] * idx[None, :]) % n) / n
    return np.cos(ang), np.sin(ang)


def _fourier_weights_kernel(cs_ref, w_ref, o_ref, *, norm):
    w = w_ref[0]
    ab = jnp.dot(cs_ref[...], w, preferred_element_type=F32, precision=lax.Precision.HIGHEST) * norm
    o_ref[0] = jnp.concatenate([ab[:HEAD_DIM], ab[HEAD_DIM:]], axis=1).astype(o_ref.dtype)


def _fourier_weights(w_f, seq):
    c, s = _dft_cos_sin(HEAD_DIM)
    cs = jnp.asarray(np.concatenate([c, s], axis=0), F32)
    norm = 1.0 / math.sqrt(seq * HEAD_DIM)
    return pl.pallas_call(
        functools.partial(_fourier_weights_kernel, norm=norm),
        grid=(B_GROUPS,),
        in_specs=[pl.BlockSpec((2 * HEAD_DIM, HEAD_DIM), lambda g: (0, 0)),
                  pl.BlockSpec((1, HEAD_DIM, HEAD_DIM), lambda g: (g, 0, 0))],
        out_specs=pl.BlockSpec((1, HEAD_DIM, 2 * HEAD_DIM), lambda g: (g, 0, 0)),
        out_shape=jax.ShapeDtypeStruct((B_GROUPS, HEAD_DIM, 2 * HEAD_DIM), BF16),
        compiler_params=_params("arbitrary"),
        name="fourier_weights",
    )(cs, w_f)


def _fourier_channel_kernel(u_ref, ab_ref, y_ref, z_ref):
    for g in range(B_GROUPS):
        hs = slice(HEAD_DIM * g, HEAD_DIM * (g + 1))
        yz = _dot(u_ref[:, hs], ab_ref[g])
        y_ref[:, hs] = yz[:, :HEAD_DIM].astype(y_ref.dtype)
        z_ref[:, hs] = yz[:, HEAD_DIM:].astype(z_ref.dtype)


def _fourier_channel(proj, ab, *, tm=1024):
    s, w = proj.shape
    ublock = (w - B_WIDTH) // B_WIDTH
    assert ublock * B_WIDTH == w - B_WIDTH
    spec = pl.BlockSpec((tm, B_WIDTH), lambda i: (i, 0))
    return pl.pallas_call(
        _fourier_channel_kernel,
        grid=(s // tm,),
        in_specs=[pl.BlockSpec((tm, B_WIDTH), lambda i: (i, ublock)),
                  pl.BlockSpec((B_GROUPS, HEAD_DIM, 2 * HEAD_DIM), lambda i: (0, 0, 0))],
        out_specs=[spec, spec],
        out_shape=[jax.ShapeDtypeStruct((s, B_WIDTH), BF16)] * 2,
        compiler_params=_params("parallel"),
        name="fourier_channel",
    )(proj, ab)


def _fourier_stage1_kernel(m1_ref, tc_ref, ts_ref, y_ref, z_ref, tre_ref, tim_ref, *, n1, n2_per_step):
    yz = jnp.concatenate([y_ref[...], z_ref[...]], axis=0)
    ab = _dot(m1_ref[...], yz)
    a, b = ab[:n1], ab[n1:]
    ch = B_WIDTH
    for q in range(n2_per_step):
        cs = slice(ch * q, ch * (q + 1))
        c = tc_ref[0, :, q:q + 1]
        s = ts_ref[0, :, q:q + 1]
        aq, bq = a[:, cs], b[:, cs]
        tre_ref[:, cs] = (aq * c + bq * s).astype(tre_ref.dtype)
        tim_ref[:, cs] = (bq * c - aq * s).astype(tim_ref.dtype)


def _fourier_stage2_kernel(m2_ref, tre_ref, tim_ref, o_ref, *, n2, k1_per_step):
    ch = B_WIDTH
    for q in range(k1_per_step):
        rs = slice(n2 * q, n2 * (q + 1))
        t = jnp.concatenate([tre_ref[rs, :], tim_ref[rs, :]], axis=0)
        o_ref[:, ch * q:ch * (q + 1)] = _dot(m2_ref[...], t).astype(o_ref.dtype)


def _fourier_position(y, z, *, n2=FFT_N2, n2_per_step=16, k1_per_step=4):
    seq, ch = y.shape
    n1 = seq // n2
    n2_per_step = min(n2_per_step, n2)
    k1_per_step = min(k1_per_step, n1)
    c1, s1 = _dft_cos_sin(n1)
    m1 = jnp.asarray(np.block([[c1, -s1], [-s1, -c1]]), BF16)
    c2, s2 = _dft_cos_sin(n2)
    m2 = jnp.asarray(np.concatenate([c2, s2], axis=1), BF16)
    k1 = np.arange(n1, dtype=np.int64)[:, None]
    nn2 = np.arange(n2, dtype=np.int64)[None, :]
    ang = 2.0 * np.pi * ((k1 * nn2) % seq) / seq
    steps = n2 // n2_per_step
    tc = jnp.asarray(np.cos(ang).reshape(n1, steps, n2_per_step).transpose(1, 0, 2), F32)
    ts = jnp.asarray(np.sin(ang).reshape(n1, steps, n2_per_step).transpose(1, 0, 2), F32)

    cols = n2_per_step * ch
    dspec = pl.BlockSpec((n1, cols), lambda t: (0, t))
    tspec = pl.BlockSpec((1, n1, n2_per_step), lambda t: (t, 0, 0))
    tre, tim = pl.pallas_call(
        functools.partial(_fourier_stage1_kernel, n1=n1, n2_per_step=n2_per_step),
        grid=(steps,),
        in_specs=[pl.BlockSpec((2 * n1, 2 * n1), lambda t: (0, 0)), tspec, tspec, dspec, dspec],
        out_specs=[dspec, dspec],
        out_shape=[jax.ShapeDtypeStruct((n1, n2 * ch), BF16)] * 2,
        compiler_params=_params("parallel"),
        name="fourier_stage1",
    )(m1, tc, ts, y.reshape(n1, n2 * ch), z.reshape(n1, n2 * ch))

    tblock = pl.BlockSpec((k1_per_step * n2, ch), lambda t: (t, 0))
    out = pl.pallas_call(
        functools.partial(_fourier_stage2_kernel, n2=n2, k1_per_step=k1_per_step),
        grid=(n1 // k1_per_step,),
        in_specs=[pl.BlockSpec((n2, 2 * n2), lambda t: (0, 0)), tblock, tblock],
        out_specs=pl.BlockSpec((n2, k1_per_step * ch), lambda t: (0, t)),
        out_shape=jax.ShapeDtypeStruct((n2, n1 * ch), BF16),
        compiler_params=_params("parallel"),
        name="fourier_stage2",
    )(m2, tre.reshape(seq, ch), tim.reshape(seq, ch))
    return out.reshape(seq, ch)


def _diff_bias_kernel(tab_ref, o_ref, *, t):
    h = pl.program_id(0)
    row = lax.broadcasted_iota(jnp.int32, (t, t), 0)
    col = lax.broadcasted_iota(jnp.int32, (t, t), 1)
    base = row - col
    for r in range(2 * DIFF_R + 1):
        d = (r - DIFF_R) * t
        o_ref[0, r] = _bias_chain(base + d, d - (t - 1), d + (t - 1),
                                  lambda b: tab_ref[b, A_HEADS + h] * LOG2_E)


def _diff_bias(table, t):
    assert DIFF_R * t - (t - 1) >= _THR[_HALF_BUCKETS - 1]
    nt = 2 * DIFF_R + 1
    return pl.pallas_call(
        functools.partial(_diff_bias_kernel, t=t),
        grid=(C_HEADS,),
        in_specs=[pl.BlockSpec(memory_space=pltpu.SMEM)],
        out_specs=pl.BlockSpec((1, nt, t, t), lambda h: (h, 0, 0, 0)),
        out_shape=jax.ShapeDtypeStruct((C_HEADS, nt, t, t), F32),
        compiler_params=_params("arbitrary"),
        name="diff_bias",
    )(table)


def _diff_attn_kernel(q_ref, k_ref, v_ref, b_ref, lam_ref, g_ref, o_ref,
                      vt_sc, s0_sc, s1_sc, p0_sc, p1_sc, alpha0_sc, alpha1_sc, m_sc, acc_sc,
                      *, t, tk, lambda_init):
    s_sc, p_sc, alpha_sc = (s0_sc, s1_sc), (p0_sc, p1_sc), (alpha0_sc, alpha1_sc)
    qg = pl.program_id(1)
    nkv = k_ref.shape[0] // tk
    ntile = q_ref.shape[0] // t
    total = ntile * nkv
    per_tile = t // tk

    @pl.when(qg == 0)
    def _():
        pad = lax.broadcasted_iota(jnp.int32, (DIFF_VT_ROWS - C_V_DIM, tk), 0)
        ones_row = jnp.where(pad == 0, 1.0, 0.0).astype(BF16)
        for j in range(nkv):
            vt_sc[j, :C_V_DIM, :] = v_ref[tk * j:tk * (j + 1), :].astype(F32).T.astype(BF16)
            vt_sc[j, C_V_DIM:, :] = ones_row

    def split(n):
        return n // nkv, n % nkv

    def key_rows(j):
        return pl.ds(pl.multiple_of(j * tk, tk), tk)

    def query_rows(tile):
        return pl.ds(pl.multiple_of(tile * t, t), t)

    def logits(n, buf):
        tile, j = split(n)
        q_tile = qg * ntile + tile
        for c in range(2):
            hs = slice(HEAD_DIM * c, HEAD_DIM * (c + 1))
            q = q_ref[query_rows(tile), hs]
            if tk <= t:
                bias_tile = jnp.clip(j // per_tile - q_tile, -DIFF_R, DIFF_R) + DIFF_R
                bias = b_ref[0, bias_tile, pl.ds(pl.multiple_of((j % per_tile) * tk, tk), tk), :]
                s_sc[buf][c] = _dot_nt(k_ref[key_rows(j), hs], q) + bias
            else:
                for r in range(tk // t):
                    bias_tile = jnp.clip(j * (tk // t) + r - q_tile, -DIFF_R, DIFF_R) + DIFF_R
                    k_rows = pl.ds(pl.multiple_of(j * tk + r * t, t), t)
                    s_sc[buf][c, t * r:t * (r + 1), :] = _dot_nt(k_ref[k_rows, hs], q) + b_ref[0, bias_tile]

    def softmax(n, buf):
        _, j = split(n)
        for c in range(2):
            m_cur = jnp.max(s_sc[buf][c], axis=0, keepdims=True)
            m_prev = jnp.where(j == 0, NEG_INF, m_sc[c])
            m_next = jnp.maximum(m_prev, m_cur)
            alpha_sc[buf][c] = jnp.exp2(m_prev - m_next)
            m_sc[c] = m_next
            p_sc[buf][c] = jnp.exp2(s_sc[buf][c] - m_next).astype(BF16)

    def values(n, buf):
        _, j = split(n)
        for c in range(2):
            acc_sc[c] = acc_sc[c] * alpha_sc[buf][c] + _dot(vt_sc[j], p_sc[buf][c])

    def finalize(tile):
        lp = lam_ref[...]
        lam = (jnp.exp(jnp.sum(lp[0:1] * lp[1:2], axis=-1, keepdims=True))
               - jnp.exp(jnp.sum(lp[2:3] * lp[3:4], axis=-1, keepdims=True)) + lambda_init)
        num0, den0 = acc_sc[0, :C_V_DIM, :], acc_sc[0, C_V_DIM:C_V_DIM + 1, :]
        num1, den1 = acc_sc[1, :C_V_DIM, :], acc_sc[1, C_V_DIM:C_V_DIM + 1, :]
        o = num0 * (1.0 / den0) - lam * (num1 * (1.0 / den1))
        ms = jnp.mean(o * o, axis=0, keepdims=True)
        o = o * lax.rsqrt(ms + NORM_EPS) * (g_ref[...] * (1.0 - lambda_init))
        o_ref[query_rows(tile), :] = o.T.astype(o_ref.dtype)

    m_sc[...] = jnp.full(m_sc.shape, NEG_INF, F32)
    acc_sc[...] = jnp.zeros(acc_sc.shape, F32)

    logits(0, 0)
    logits(1, 1)
    softmax(0, 0)

    def body(i, carry):
        n = 2 * i + 1
        logits(n + 1, 0)
        softmax(n, 1)
        values(n - 1, 0)
        logits(n + 2, 1)
        softmax(n + 1, 0)
        values(n, 1)

        @pl.when((n + 1) % nkv == 0)
        def _():
            finalize(n // nkv)

        return carry

    lax.fori_loop(0, total // 2 - 1, body, 0)
    softmax(total - 1, 1)
    values(total - 2, 0)
    values(total - 1, 1)
    finalize(ntile - 1)


def _diff_attn(proj, bias, lam_params, subln_g, lambda_init, *, t, tk=DIFF_TK, tiles_per_step=DIFF_TILES_PER_STEP):
    s = proj.shape[0]
    nt = 2 * DIFF_R + 1
    tk = min(tk, s // 4)
    nkv = s // tk
    tiles_per_step = min(tiles_per_step, s // t)
    tq = tiles_per_step * t
    assert nkv * tk == s and nkv % 2 == 0 and nkv >= 4 and (t % tk == 0 or tk % t == 0) and s % tq == 0
    return pl.pallas_call(
        functools.partial(_diff_attn_kernel, t=t, tk=tk, lambda_init=lambda_init),
        grid=(C_HEADS, s // tq),
        in_specs=[pl.BlockSpec((tq, C_V_DIM), lambda h, i: (i, h)),
                  pl.BlockSpec((s, C_V_DIM), lambda h, i: (0, C_HEADS + h)),
                  pl.BlockSpec((s, C_V_DIM), lambda h, i: (0, 2 * C_HEADS + h)),
                  pl.BlockSpec((1, nt, t, t), lambda h, i: (h, 0, 0, 0)),
                  pl.BlockSpec((4, HEAD_DIM), lambda h, i: (0, 0)),
                  pl.BlockSpec((C_V_DIM, 1), lambda h, i: (0, 0))],
        out_specs=pl.BlockSpec((tq, C_V_DIM), lambda h, i: (i, h)),
        out_shape=jax.ShapeDtypeStruct((s, C_HEADS * C_V_DIM), BF16),
        scratch_shapes=[pltpu.VMEM((nkv, DIFF_VT_ROWS, tk), BF16),
                        pltpu.VMEM((2, tk, t), F32), pltpu.VMEM((2, tk, t), F32),
                        pltpu.VMEM((2, tk, t), BF16), pltpu.VMEM((2, tk, t), BF16),
                        pltpu.VMEM((2, 1, t), F32), pltpu.VMEM((2, 1, t), F32), pltpu.VMEM((2, 1, t), F32),
                        pltpu.VMEM((2, DIFF_VT_ROWS, t), F32)],
        compiler_params=_params("arbitrary", "arbitrary"),
        name="diff_attn",
    )(proj, proj, proj, bias, lam_params, subln_g.reshape(C_V_DIM, 1))


def kernel(x, norm_mix_g, norm_ffn_g, norm_final_g, rel_bias_table, w_in_even, w_fnet, w_out_even,
           w_qkv_odd, lambda_q1, lambda_k1, lambda_q2, lambda_k2, subln_g, w_out_odd, w_ff1, w_ff2):
    batch, seq, d_model = x.shape
    depth = norm_mix_g.shape[0]
    table = rel_bias_table.astype(F32)
    dil_bias = _dilated_bias(table)
    diff_t = min(DIFF_T, seq)
    diff_bias = _diff_bias(table, diff_t)
    c_qk_width = C_HEADS * 2 * HEAD_DIM
    odd_scale = jnp.concatenate([jnp.full((1, c_qk_width), LOG2_E / math.sqrt(HEAD_DIM), F32),
                                 jnp.ones((1, w_qkv_odd.shape[2] - c_qk_width), F32)], axis=1)
    even_scale = jnp.concatenate([jnp.full((1, A_WIDTH), LOG2_E / math.sqrt(HEAD_DIM), F32),
                                  jnp.ones((1, w_in_even.shape[2] - A_WIDTH), F32)], axis=1)

    w_in_even_b, w_qkv_odd_b = w_in_even.astype(BF16), w_qkv_odd.astype(BF16)
    w_ff1_b, w_ff2_b = w_ff1.astype(BF16), w_ff2.astype(BF16)
    w_out_even_f, w_out_odd_f = w_out_even.astype(F32), w_out_odd.astype(F32)

    outs = []
    for bidx in range(batch):
        xs = x.reshape(seq, d_model) if batch == 1 else x[bidx]
        for i in range(depth):
            j = i // 2
            if i % 2 == 0:
                class_dils = tuple(dil for _, dil in DILATED_CONFIGS if dil > 1)
                proj, *by_class = _norm_matmul(xs, norm_mix_g[i], w_in_even_b, j, even_scale,
                                               class_dils=class_dils)
                by_class = dict(zip(class_dils, by_class))
                by_class[1] = proj.reshape(1, *proj.shape)
                branches = [_dilated_branch(by_class[dil], dil_bias[bi])
                            for bi, (_, dil) in enumerate(DILATED_CONFIGS)]
                y, z = _fourier_channel(proj, _fourier_weights(w_fnet[j], seq))
                fb = _fourier_position(y, z)
                mixed = _dilated_merge([b[0] for b in branches], [b[1] for b in branches], fb)
                xs = _matmul_residual(mixed, w_out_even_f, j, xs)
            else:
                lambda_init = 0.8 - 0.6 * math.exp(-0.3 * i)
                proj, = _norm_matmul(xs, norm_mix_g[i], w_qkv_odd_b, j, odd_scale)
                lam_params = jnp.stack([lambda_q1[j], lambda_k1[j], lambda_q2[j], lambda_k2[j]]).astype(F32)
                attn = _diff_attn(proj, diff_bias, lam_params, subln_g[j], lambda_init, t=diff_t)
                xs = _matmul_residual(attn, w_out_odd_f, j, xs)
            xs = _mlp(xs, norm_ffn_g[i], w_ff1_b, w_ff2_b, i, norm_final_g, final_norm=(i == depth - 1))
        outs.append(xs)
    return outs[0].reshape(1, seq, d_model) if batch == 1 else jnp.stack(outs, axis=0)
```

```python
import functools
import math

import numpy as np
import jax
import jax.numpy as jnp
from jax import lax
from jax.experimental import pallas as pl
from jax.experimental.pallas import tpu as pltpu

F32 = jnp.float32
BF16 = jnp.bfloat16

HEAD_DIM = 128
A_HEADS = 12
B_GROUPS = 4
C_HEADS = 8
DILATED_CONFIGS = ((128, 1), (512, 4), (2048, 16))
NUM_BUCKETS = 32
MAX_DISTANCE = 1024
NORM_EPS = 1e-6
NEG_INF = -1e30
LOG2_E = math.log2(math.e)

A_WIDTH = A_HEADS * HEAD_DIM
B_WIDTH = B_GROUPS * HEAD_DIM
C_V_DIM = 2 * HEAD_DIM
HALF_STEPS = 64
assert all(w // (2 * d) == HALF_STEPS for w, d in DILATED_CONFIGS)

LANES = 128
V7X_VMEM_LIMIT_BYTES = 58 * 1024 * 1024

DIL_TQ = 1024
DIL_SUB = 128
DIL_BAND = DIL_SUB + 2 * HALF_STEPS
DIL_HG = 4
DIFF_T = 512
DIFF_TK = 512
DIFF_TILES_PER_STEP = 8
DIFF_R = 3
BF16_SUBLANES = 16
DIFF_VT_ROWS = C_V_DIM + BF16_SUBLANES
FFT_N2 = 128


def _params(*sem, flags=None):
    return pltpu.CompilerParams(dimension_semantics=sem, vmem_limit_bytes=V7X_VMEM_LIMIT_BYTES, flags=flags)


def _magnitude_thresholds():
    half = NUM_BUCKETS // 2
    max_exact = half // 2
    n = np.arange(1, 4 * MAX_DISTANCE, dtype=np.int64)
    large = max_exact + (np.log(n / max_exact) / math.log(MAX_DISTANCE / max_exact)
                         * (half - max_exact)).astype(np.int64)
    large = np.minimum(large, half - 1)
    bucket = np.where(n < max_exact, n, large)
    assert np.all(np.diff(bucket) >= 0)
    thr = {b: int(n[np.argmax(bucket >= b)]) for b in range(1, half)}
    return thr


_THR = _magnitude_thresholds()
_HALF_BUCKETS = NUM_BUCKETS // 2


def _mag_bucket(n):
    return sum(1 for b in range(1, _HALF_BUCKETS) if n >= _THR[b])


def _bias_chain(rel, lo, hi, tab):
    n = jnp.abs(rel)

    def side(base, nlo, nhi):
        bmin, bmax = _mag_bucket(nlo), _mag_bucket(nhi)
        val = jnp.full(rel.shape, tab(base + bmax), F32)
        for b in range(bmax - 1, bmin - 1, -1):
            val = jnp.where(n < _THR[b + 1], tab(base + b), val)
        return val

    if lo > 0:
        return side(_HALF_BUCKETS, lo, hi)
    if hi <= 0:
        return side(0, -hi, -lo)
    return jnp.where(rel > 0, side(_HALF_BUCKETS, 1, hi), side(0, 0, -lo))


def _rms(xf, g):
    ms = jnp.mean(xf * xf, axis=-1, keepdims=True)
    return xf * lax.rsqrt(ms + NORM_EPS) * g


def _dot(a, b):
    return jnp.dot(a, b, preferred_element_type=F32)


def _dot_nt(a, b):
    return lax.dot_general(a, b, (((1,), (1,)), ((), ())), preferred_element_type=F32)


def _norm_matmul_kernel(x_ref, g_ref, w_ref, cs_ref, o_ref, *rest, class_dils):
    class_refs = rest[:len(class_dils)]
    hn_ref = rest[len(class_dils)]

    @pl.when(pl.program_id(1) == 0)
    def _():
        hn_ref[...] = _rms(x_ref[...], g_ref[...]).astype(BF16)

    y = _dot(hn_ref[...], w_ref[...]) * cs_ref[...]
    o_ref[...] = y.astype(o_ref.dtype)

    if class_dils:
        y_sc = rest[len(class_dils) + 1]
        tm, tn = y.shape
        for c in range(tn // LANES):
            y_sc[c] = y[:, LANES * c:LANES * (c + 1)]
        for ref, dil in zip(class_refs, class_dils):
            for r in range(dil):
                for c in range(tn // LANES):
                    ref[r, :, LANES * c:LANES * (c + 1)] = (
                        y_sc[c, pl.ds(r, tm // dil, stride=dil), :].astype(ref.dtype))


def _norm_matmul(x, g, w, layer, col_scale, *, class_dils=(), tm=1024, tn=1024):
    m, d = x.shape
    n = w.shape[2]
    tm = min(tm, m)
    out_specs = [pl.BlockSpec((tm, tn), lambda i, j: (i, j))]
    out_shape = [jax.ShapeDtypeStruct((m, n), BF16)]
    scratch = [pltpu.VMEM((tm, d), BF16)]
    for dil in class_dils:
        out_specs.append(pl.BlockSpec((dil, tm // dil, tn), lambda i, j: (0, i, j)))
        out_shape.append(jax.ShapeDtypeStruct((dil, m // dil, n), BF16))
    if class_dils:
        scratch.append(pltpu.VMEM((tn // LANES, tm, LANES), F32))
    return pl.pallas_call(
        functools.partial(_norm_matmul_kernel, class_dils=tuple(class_dils)),
        grid=(m // tm, n // tn),
        in_specs=[pl.BlockSpec((tm, d), lambda i, j: (i, 0)),
                  pl.BlockSpec((1, d), lambda i, j: (0, 0)),
                  pl.BlockSpec((None, d, tn), lambda i, j: (layer, 0, j)),
                  pl.BlockSpec((1, tn), lambda i, j: (0, j))],
        out_specs=out_specs,
        out_shape=out_shape,
        scratch_shapes=scratch,
        compiler_params=_params("parallel", "arbitrary"),
        name="norm_matmul",
    )(x, g.reshape(1, d), w, col_scale)


def _matmul_residual_kernel(a_ref, w_ref, r_ref, o_ref, wb_ref):
    @pl.when(pl.program_id(0) == 0)
    def _():
        wb_ref[...] = w_ref[...].astype(BF16)

    o_ref[...] = r_ref[...] + _dot(a_ref[...], wb_ref[...])


def _matmul_residual(a, w, layer, res, *, tm=512):
    m, k = a.shape
    n = w.shape[2]
    tm = min(tm, m)
    return pl.pallas_call(
        _matmul_residual_kernel,
        grid=(m // tm,),
        in_specs=[pl.BlockSpec((tm, k), lambda i: (i, 0)),
                  pl.BlockSpec((None, k, n), lambda i: (layer, 0, 0), pipeline_mode=pl.Buffered(1)),
                  pl.BlockSpec((tm, n), lambda i: (i, 0))],
        out_specs=pl.BlockSpec((tm, n), lambda i: (i, 0)),
        out_shape=jax.ShapeDtypeStruct((m, n), F32),
        scratch_shapes=[pltpu.VMEM((k, n), BF16)],
        compiler_params=_params("arbitrary"),
        name="matmul_residual",
    )(a, w, res)


def _mlp_kernel(x_ref, g_ref, w1_ref, w2_ref, gf_ref, o_ref, hn_ref, *, final_norm):
    c = pl.program_id(1)

    @pl.when(c == 0)
    def _():
        x = x_ref[...]
        hn_ref[...] = _rms(x, g_ref[...]).astype(BF16)
        o_ref[...] = x

    u = jnp.maximum(_dot(hn_ref[...], w1_ref[...]), 0.0)
    o_ref[...] += _dot((u * u).astype(BF16), w2_ref[...])

    if final_norm:
        @pl.when(c == pl.num_programs(1) - 1)
        def _():
            o_ref[...] = _rms(o_ref[...], gf_ref[...])


def _mlp(x, g, w1, w2, layer, gf, *, final_norm, tm=1024, tf=1024):
    m, d = x.shape
    f = w1.shape[2]
    tm = min(tm, m)
    return pl.pallas_call(
        functools.partial(_mlp_kernel, final_norm=final_norm),
        grid=(m // tm, f // tf),
        in_specs=[pl.BlockSpec((tm, d), lambda i, c: (i, 0)),
                  pl.BlockSpec((1, d), lambda i, c: (0, 0)),
                  pl.BlockSpec((None, d, tf), lambda i, c: (layer, 0, c)),
                  pl.BlockSpec((None, tf, d), lambda i, c: (layer, c, 0)),
                  pl.BlockSpec((1, d), lambda i, c: (0, 0))],
        out_specs=pl.BlockSpec((tm, d), lambda i, c: (i, 0)),
        out_shape=jax.ShapeDtypeStruct((m, d), F32),
        scratch_shapes=[pltpu.VMEM((tm, d), BF16)],
        compiler_params=_params("parallel", "arbitrary"),
        name="mlp",
    )(x, g.reshape(1, d), w1, w2, gf.reshape(1, d))


def _dilated_bias_kernel(tab_ref, o_ref):
    h = pl.program_id(0)
    row = lax.broadcasted_iota(jnp.int32, (DIL_SUB, DIL_BAND), 0)
    col = lax.broadcasted_iota(jnp.int32, (DIL_SUB, DIL_BAND), 1)
    steps = col - row - HALF_STEPS
    in_band = jnp.abs(steps) <= HALF_STEPS
    for bi, (_, dil) in enumerate(DILATED_CONFIGS):
        reach = HALF_STEPS * dil
        rel = jnp.clip(steps, -HALF_STEPS, HALF_STEPS) * dil
        bias = _bias_chain(rel, -reach, reach, lambda b: tab_ref[b, h] * LOG2_E)
        o_ref[bi, 0] = jnp.where(in_band, bias, NEG_INF)


def _dilated_bias(table):
    nb = len(DILATED_CONFIGS)
    return pl.pallas_call(
        _dilated_bias_kernel,
        grid=(A_HEADS,),
        in_specs=[pl.BlockSpec(memory_space=pltpu.SMEM)],
        out_specs=pl.BlockSpec((nb, 1, DIL_SUB, DIL_BAND), lambda h: (0, h, 0, 0)),
        out_shape=jax.ShapeDtypeStruct((nb, A_HEADS, DIL_SUB, DIL_BAND), F32),
        compiler_params=_params("arbitrary"),
        name="dilated_bias",
    )(table)


def _dilated_branch_kernel(q_ref, kp_ref, kc_ref, kn_ref, vp_ref, vc_ref, vn_ref, b_ref,
                           o_ref, lse_ref, *, tq):
    tb = pl.program_id(2)
    last_tb = pl.num_programs(2) - 1
    nsub = tq // DIL_SUB
    col = lax.broadcasted_iota(jnp.int32, (DIL_SUB, DIL_BAND), 1)
    lane = lax.broadcasted_iota(jnp.int32, (DIL_SUB, LANES), 1)
    lanes_per_head = LANES // DIL_HG

    def band(prev_ref, cur_ref, next_ref, i, hs):
        lo = DIL_SUB * i - HALF_STEPS
        hi = lo + DIL_BAND
        parts = []
        if lo < 0:
            parts.append(prev_ref[:, hs])
        parts.append(cur_ref[max(lo, 0):min(hi, tq), hs])
        if hi > tq:
            parts.append(next_ref[:, hs])
        return parts[0] if len(parts) == 1 else jnp.concatenate(parts, axis=0)

    for i in range(nsub):
        lse_blk = None
        for j in range(DIL_HG):
            hs = slice(HEAD_DIM * j, HEAD_DIM * (j + 1))
            q = q_ref[DIL_SUB * i:DIL_SUB * (i + 1), hs]
            kb = band(kp_ref, kc_ref, kn_ref, i, hs)
            vb = band(vp_ref, vc_ref, vn_ref, i, hs)
            s = _dot_nt(q, kb) + b_ref[j]
            if i == 0:
                s = jnp.where((tb > 0) | (col >= HALF_STEPS), s, NEG_INF)
            if i == nsub - 1:
                s = jnp.where((tb < last_tb) | (col < DIL_BAND - HALF_STEPS), s, NEG_INF)
            m = jnp.max(s, axis=-1, keepdims=True)
            p = jnp.exp2(s - m)
            den = jnp.sum(p, axis=-1, keepdims=True)
            o = _dot(p.astype(BF16), vb) / den
            o_ref[DIL_SUB * i:DIL_SUB * (i + 1), hs] = o.astype(o_ref.dtype)
            lse = m + jnp.log2(den)
            lse_blk = lse if lse_blk is None else jnp.where(lane >= lanes_per_head * j, lse, lse_blk)
        lse_ref[DIL_SUB * i:DIL_SUB * (i + 1), :] = jnp.broadcast_to(lse_blk, (DIL_SUB, LANES))


def _dilated_branch(proj_c, bias_b, *, tq=DIL_TQ):
    dil, l, w = proj_c.shape
    tq = min(tq, l)
    ngroups = A_HEADS // DIL_HG
    gw = DIL_HG * HEAD_DIM
    qoff, koff, voff = 0, A_WIDTH // gw, 2 * A_WIDTH // gw
    halo_per_tq = tq // HALF_STEPS
    n_halo = l // HALF_STEPS

    def cur(off):
        return pl.BlockSpec((None, tq, gw), lambda r, g, t: (r, t, off + g))

    def prev(off):
        return pl.BlockSpec((None, HALF_STEPS, gw),
                            lambda r, g, t: (r, jnp.maximum(t * halo_per_tq - 1, 0), off + g))

    def nxt(off):
        return pl.BlockSpec((None, HALF_STEPS, gw),
                            lambda r, g, t: (r, jnp.minimum((t + 1) * halo_per_tq, n_halo - 1), off + g))

    return pl.pallas_call(
        functools.partial(_dilated_branch_kernel, tq=tq),
        grid=(dil, ngroups, l // tq),
        in_specs=[cur(qoff), prev(koff), cur(koff), nxt(koff), prev(voff), cur(voff), nxt(voff),
                  pl.BlockSpec((DIL_HG, DIL_SUB, DIL_BAND), lambda r, g, t: (g, 0, 0))],
        out_specs=[pl.BlockSpec((None, tq, gw), lambda r, g, t: (r, t, g)),
                   pl.BlockSpec((None, tq, LANES), lambda r, g, t: (r, t, g))],
        out_shape=[jax.ShapeDtypeStruct((dil, l, A_WIDTH), BF16),
                   jax.ShapeDtypeStruct((dil, l, ngroups * LANES), F32)],
        compiler_params=_params("parallel", "parallel", "arbitrary"),
        name=f"dilated_branch_d{dil}",
    )(*([proj_c] * 7), bias_b)


def _dilated_merge_kernel(*refs, dils):
    nb = len(dils)
    o_refs, l_refs = refs[:nb], refs[nb:2 * nb]
    fb_ref, out_ref, l_sc, o_sc = refs[2 * nb:]
    g = pl.program_id(1)
    ngroups = A_HEADS // DIL_HG
    lanes_per_head = LANES // DIL_HG
    tm = out_ref.shape[0]

    @pl.when(g < ngroups)
    def _():
        for b, dil in enumerate(dils):
            for r in range(dil):
                rows = pl.ds(r, tm // dil, stride=dil)
                l_sc[b, rows, :] = l_refs[b][r]
                for j in range(DIL_HG):
                    o_sc[b, j, rows, :] = o_refs[b][r, :, HEAD_DIM * j:HEAD_DIM * (j + 1)].astype(F32)
        ls = [l_sc[b] for b in range(nb)]
        mx = functools.reduce(jnp.maximum, ls)
        es = [jnp.exp2(l - mx) for l in ls]
        inv = 1.0 / functools.reduce(jnp.add, es)
        for j in range(DIL_HG):
            lc = slice(lanes_per_head * j, lanes_per_head * j + 1)
            acc = functools.reduce(jnp.add, [(es[b] * inv)[:, lc] * o_sc[b, j] for b in range(nb)])
            out_ref[:, HEAD_DIM * j:HEAD_DIM * (j + 1)] = acc.astype(out_ref.dtype)

    @pl.when(g == ngroups)
    def _():
        out_ref[...] = fb_ref[...]


def _dilated_merge(os_, lses, fb, *, tm=1024):
    s = fb.shape[0]
    tm = min(tm, s)
    ngroups = A_HEADS // DIL_HG
    gw = DIL_HG * HEAD_DIM
    assert fb.shape[1] == gw and HEAD_DIM == LANES
    dils = tuple(o.shape[0] for o in os_)
    clamp = lambda i, g: (0, i, jnp.minimum(g, ngroups - 1))
    return pl.pallas_call(
        functools.partial(_dilated_merge_kernel, dils=dils),
        grid=(s // tm, ngroups + 1),
        in_specs=[pl.BlockSpec((d, tm // d, gw), clamp) for d in dils]
                 + [pl.BlockSpec((d, tm // d, LANES), clamp) for d in dils]
                 + [pl.BlockSpec((tm, gw), lambda i, g: (i, 0))],
        out_specs=pl.BlockSpec((tm, gw), lambda i, g: (i, g)),
        out_shape=jax.ShapeDtypeStruct((s, A_WIDTH + B_WIDTH), BF16),
        scratch_shapes=[pltpu.VMEM((len(dils), tm, LANES), F32),
                        pltpu.VMEM((len(dils), DIL_HG, tm, LANES), F32)],
        compiler_params=_params("parallel", "arbitrary"),
        name="dilated_merge",
    )(*os_, *lses, fb)


def _dft_cos_sin(n):
    idx = np.arange(n, dtype=np.int64)
    ang = 2.0 * np.pi * ((idx[:, None] * idx[None, :]) % n) / n
    return np.cos(ang), np.sin(ang)


def _fourier_weights_kernel(cs_ref, w_ref, o_ref, *, norm):
    w = w_ref[0]
    ab = jnp.dot(cs_ref[...], w, preferred_element_type=F32, precision=lax.Precision.HIGHEST) * norm
    o_ref[0] = jnp.concatenate([ab[:HEAD_DIM], ab[HEAD_DIM:]], axis=1).astype(o_ref.dtype)


def _fourier_weights(w_f, seq):
    c, s = _dft_cos_sin(HEAD_DIM)
    cs = jnp.asarray(np.concatenate([c, s], axis=0), F32)
    norm = 1.0 / math.sqrt(seq * HEAD_DIM)
    return pl.pallas_call(
        functools.partial(_fourier_weights_kernel, norm=norm),
        grid=(B_GROUPS,),
        in_specs=[pl.BlockSpec((2 * HEAD_DIM, HEAD_DIM), lambda g: (0, 0)),
                  pl.BlockSpec((1, HEAD_DIM, HEAD_DIM), lambda g: (g, 0, 0))],
        out_specs=pl.BlockSpec((1, HEAD_DIM, 2 * HEAD_DIM), lambda g: (g, 0, 0)),
        out_shape=jax.ShapeDtypeStruct((B_GROUPS, HEAD_DIM, 2 * HEAD_DIM), BF16),
        compiler_params=_params("arbitrary"),
        name="fourier_weights",
    )(cs, w_f)


def _fourier_channel_kernel(u_ref, ab_ref, y_ref, z_ref):
    for g in range(B_GROUPS):
        hs = slice(HEAD_DIM * g, HEAD_DIM * (g + 1))
        yz = _dot(u_ref[:, hs], ab_ref[g])
        y_ref[:, hs] = yz[:, :HEAD_DIM].astype(y_ref.dtype)
        z_ref[:, hs] = yz[:, HEAD_DIM:].astype(z_ref.dtype)


def _fourier_channel(proj, ab, *, tm=1024):
    s, w = proj.shape
    ublock = (w - B_WIDTH) // B_WIDTH
    assert ublock * B_WIDTH == w - B_WIDTH
    spec = pl.BlockSpec((tm, B_WIDTH), lambda i: (i, 0))
    return pl.pallas_call(
        _fourier_channel_kernel,
        grid=(s // tm,),
        in_specs=[pl.BlockSpec((tm, B_WIDTH), lambda i: (i, ublock)),
                  pl.BlockSpec((B_GROUPS, HEAD_DIM, 2 * HEAD_DIM), lambda i: (0, 0, 0))],
        out_specs=[spec, spec],
        out_shape=[jax.ShapeDtypeStruct((s, B_WIDTH), BF16)] * 2,
        compiler_params=_params("parallel"),
        name="fourier_channel",
    )(proj, ab)


def _fourier_stage1_kernel(m1_ref, tc_ref, ts_ref, y_ref, z_ref, tre_ref, tim_ref, *, n1, n2_per_step):
    yz = jnp.concatenate([y_ref[...], z_ref[...]], axis=0)
    ab = _dot(m1_ref[...], yz)
    a, b = ab[:n1], ab[n1:]
    ch = B_WIDTH
    for q in range(n2_per_step):
        cs = slice(ch * q, ch * (q + 1))
        c = tc_ref[0, :, q:q + 1]
        s = ts_ref[0, :, q:q + 1]
        aq, bq = a[:, cs], b[:, cs]
        tre_ref[:, cs] = (aq * c + bq * s).astype(tre_ref.dtype)
        tim_ref[:, cs] = (bq * c - aq * s).astype(tim_ref.dtype)


def _fourier_stage2_kernel(m2_ref, tre_ref, tim_ref, o_ref, *, n2, k1_per_step):
    ch = B_WIDTH
    for q in range(k1_per_step):
        rs = slice(n2 * q, n2 * (q + 1))
        t = jnp.concatenate([tre_ref[rs, :], tim_ref[rs, :]], axis=0)
        o_ref[:, ch * q:ch * (q + 1)] = _dot(m2_ref[...], t).astype(o_ref.dtype)


def _fourier_position(y, z, *, n2=FFT_N2, n2_per_step=16, k1_per_step=4):
    seq, ch = y.shape
    n1 = seq // n2
    n2_per_step = min(n2_per_step, n2)
    k1_per_step = min(k1_per_step, n1)
    c1, s1 = _dft_cos_sin(n1)
    m1 = jnp.asarray(np.block([[c1, -s1], [-s1, -c1]]), BF16)
    c2, s2 = _dft_cos_sin(n2)
    m2 = jnp.asarray(np.concatenate([c2, s2], axis=1), BF16)
    k1 = np.arange(n1, dtype=np.int64)[:, None]
    nn2 = np.arange(n2, dtype=np.int64)[None, :]
    ang = 2.0 * np.pi * ((k1 * nn2) % seq) / seq
    steps = n2 // n2_per_step
    tc = jnp.asarray(np.cos(ang).reshape(n1, steps, n2_per_step).transpose(1, 0, 2), F32)
    ts = jnp.asarray(np.sin(ang).reshape(n1, steps, n2_per_step).transpose(1, 0, 2), F32)

    cols = n2_per_step * ch
    dspec = pl.BlockSpec((n1, cols), lambda t: (0, t))
    tspec = pl.BlockSpec((1, n1, n2_per_step), lambda t: (t, 0, 0))
    tre, tim = pl.pallas_call(
        functools.partial(_fourier_stage1_kernel, n1=n1, n2_per_step=n2_per_step),
        grid=(steps,),
        in_specs=[pl.BlockSpec((2 * n1, 2 * n1), lambda t: (0, 0)), tspec, tspec, dspec, dspec],
        out_specs=[dspec, dspec],
        out_shape=[jax.ShapeDtypeStruct((n1, n2 * ch), BF16)] * 2,
        compiler_params=_params("parallel"),
        name="fourier_stage1",
    )(m1, tc, ts, y.reshape(n1, n2 * ch), z.reshape(n1, n2 * ch))

    tblock = pl.BlockSpec((k1_per_step * n2, ch), lambda t: (t, 0))
    out = pl.pallas_call(
        functools.partial(_fourier_stage2_kernel, n2=n2, k1_per_step=k1_per_step),
        grid=(n1 // k1_per_step,),
        in_specs=[pl.BlockSpec((n2, 2 * n2), lambda t: (0, 0)), tblock, tblock],
        out_specs=pl.BlockSpec((n2, k1_per_step * ch), lambda t: (0, t)),
        out_shape=jax.ShapeDtypeStruct((n2, n1 * ch), BF16),
        compiler_params=_params("parallel"),
        name="fourier_stage2",
    )(m2, tre.reshape(seq, ch), tim.reshape(seq, ch))
    return out.reshape(seq, ch)


def _diff_bias_kernel(tab_ref, o_ref, *, t):
    h = pl.program_id(0)
    row = lax.broadcasted_iota(jnp.int32, (t, t), 0)
    col = lax.broadcasted_iota(jnp.int32, (t, t), 1)
    base = row - col
    for r in range(2 * DIFF_R + 1):
        d = (r - DIFF_R) * t
        o_ref[0, r] = _bias_chain(base + d, d - (t - 1), d + (t - 1),
                                  lambda b: tab_ref[b, A_HEADS + h] * LOG2_E)


def _diff_bias(table, t):
    assert DIFF_R * t - (t - 1) >= _THR[_HALF_BUCKETS - 1]
    nt = 2 * DIFF_R + 1
    return pl.pallas_call(
        functools.partial(_diff_bias_kernel, t=t),
        grid=(C_HEADS,),
        in_specs=[pl.BlockSpec(memory_space=pltpu.SMEM)],
        out_specs=pl.BlockSpec((1, nt, t, t), lambda h: (h, 0, 0, 0)),
        out_shape=jax.ShapeDtypeStruct((C_HEADS, nt, t, t), F32),
        compiler_params=_params("arbitrary"),
        name="diff_bias",
    )(table)


def _diff_attn_kernel(q_ref, k_ref, v_ref, b_ref, lam_ref, g_ref, o_ref,
                      vt_sc, s0_sc, s1_sc, p0_sc, p1_sc, alpha0_sc, alpha1_sc, m_sc, acc_sc,
                      *, t, tk, lambda_init):
    s_sc, p_sc, alpha_sc = (s0_sc, s1_sc), (p0_sc, p1_sc), (alpha0_sc, alpha1_sc)
    qg = pl.program_id(1)
    nkv = k_ref.shape[0] // tk
    ntile = q_ref.shape[0] // t
    total = ntile * nkv
    per_tile = t // tk

    @pl.when(qg == 0)
    def _():
        pad = lax.broadcasted_iota(jnp.int32, (DIFF_VT_ROWS - C_V_DIM, tk), 0)
        ones_row = jnp.where(pad == 0, 1.0, 0.0).astype(BF16)
        for j in range(nkv):
            vt_sc[j, :C_V_DIM, :] = v_ref[tk * j:tk * (j + 1), :].astype(F32).T.astype(BF16)
            vt_sc[j, C_V_DIM:, :] = ones_row

    def split(n):
        return n // nkv, n % nkv

    def key_rows(j):
        return pl.ds(pl.multiple_of(j * tk, tk), tk)

    def query_rows(tile):
        return pl.ds(pl.multiple_of(tile * t, t), t)

    def logits(n, buf):
        tile, j = split(n)
        q_tile = qg * ntile + tile
        for c in range(2):
            hs = slice(HEAD_DIM * c, HEAD_DIM * (c + 1))
            q = q_ref[query_rows(tile), hs]
            if tk <= t:
                bias_tile = jnp.clip(j // per_tile - q_tile, -DIFF_R, DIFF_R) + DIFF_R
                bias = b_ref[0, bias_tile, pl.ds(pl.multiple_of((j % per_tile) * tk, tk), tk), :]
                s_sc[buf][c] = _dot_nt(k_ref[key_rows(j), hs], q) + bias
            else:
                for r in range(tk // t):
                    bias_tile = jnp.clip(j * (tk // t) + r - q_tile, -DIFF_R, DIFF_R) + DIFF_R
                    k_rows = pl.ds(pl.multiple_of(j * tk + r * t, t), t)
                    s_sc[buf][c, t * r:t * (r + 1), :] = _dot_nt(k_ref[k_rows, hs], q) + b_ref[0, bias_tile]

    def softmax(n, buf):
        _, j = split(n)
        for c in range(2):
            m_cur = jnp.max(s_sc[buf][c], axis=0, keepdims=True)
            m_prev = jnp.where(j == 0, NEG_INF, m_sc[c])
            m_next = jnp.maximum(m_prev, m_cur)
            alpha_sc[buf][c] = jnp.exp2(m_prev - m_next)
            m_sc[c] = m_next
            p_sc[buf][c] = jnp.exp2(s_sc[buf][c] - m_next).astype(BF16)

    def values(n, buf):
        _, j = split(n)
        for c in range(2):
            acc_sc[c] = acc_sc[c] * alpha_sc[buf][c] + _dot(vt_sc[j], p_sc[buf][c])

    def finalize(tile):
        lp = lam_ref[...]
        lam = (jnp.exp(jnp.sum(lp[0:1] * lp[1:2], axis=-1, keepdims=True))
               - jnp.exp(jnp.sum(lp[2:3] * lp[3:4], axis=-1, keepdims=True)) + lambda_init)
        num0, den0 = acc_sc[0, :C_V_DIM, :], acc_sc[0, C_V_DIM:C_V_DIM + 1, :]
        num1, den1 = acc_sc[1, :C_V_DIM, :], acc_sc[1, C_V_DIM:C_V_DIM + 1, :]
        o = num0 * (1.0 / den0) - lam * (num1 * (1.0 / den1))
        ms = jnp.mean(o * o, axis=0, keepdims=True)
        o = o * lax.rsqrt(ms + NORM_EPS) * (g_ref[...] * (1.0 - lambda_init))
        o_ref[query_rows(tile), :] = o.T.astype(o_ref.dtype)

    m_sc[...] = jnp.full(m_sc.shape, NEG_INF, F32)
    acc_sc[...] = jnp.zeros(acc_sc.shape, F32)

    logits(0, 0)
    logits(1, 1)
    softmax(0, 0)

    def body(i, carry):
        n = 2 * i + 1
        logits(n + 1, 0)
        softmax(n, 1)
        values(n - 1, 0)
        logits(n + 2, 1)
        softmax(n + 1, 0)
        values(n, 1)

        @pl.when((n + 1) % nkv == 0)
        def _():
            finalize(n // nkv)

        return carry

    lax.fori_loop(0, total // 2 - 1, body, 0)
    softmax(total - 1, 1)
    values(total - 2, 0)
    values(total - 1, 1)
    finalize(ntile - 1)


def _diff_attn(proj, bias, lam_params, subln_g, lambda_init, *, t, tk=DIFF_TK, tiles_per_step=DIFF_TILES_PER_STEP):
    s = proj.shape[0]
    nt = 2 * DIFF_R + 1
    tk = min(tk, s // 4)
    nkv = s // tk
    tiles_per_step = min(tiles_per_step, s // t)
    tq = tiles_per_step * t
    assert nkv * tk == s and nkv % 2 == 0 and nkv >= 4 and (t % tk == 0 or tk % t == 0) and s % tq == 0
    return pl.pallas_call(
        functools.partial(_diff_attn_kernel, t=t, tk=tk, lambda_init=lambda_init),
        grid=(C_HEADS, s // tq),
        in_specs=[pl.BlockSpec((tq, C_V_DIM), lambda h, i: (i, h)),
                  pl.BlockSpec((s, C_V_DIM), lambda h, i: (0, C_HEADS + h)),
                  pl.BlockSpec((s, C_V_DIM), lambda h, i: (0, 2 * C_HEADS + h)),
                  pl.BlockSpec((1, nt, t, t), lambda h, i: (h, 0, 0, 0)),
                  pl.BlockSpec((4, HEAD_DIM), lambda h, i: (0, 0)),
                  pl.BlockSpec((C_V_DIM, 1), lambda h, i: (0, 0))],
        out_specs=pl.BlockSpec((tq, C_V_DIM), lambda h, i: (i, h)),
        out_shape=jax.ShapeDtypeStruct((s, C_HEADS * C_V_DIM), BF16),
        scratch_shapes=[pltpu.VMEM((nkv, DIFF_VT_ROWS, tk), BF16),
                        pltpu.VMEM((2, tk, t), F32), pltpu.VMEM((2, tk, t), F32),
                        pltpu.VMEM((2, tk, t), BF16), pltpu.VMEM((2, tk, t), BF16),
                        pltpu.VMEM((2, 1, t), F32), pltpu.VMEM((2, 1, t), F32), pltpu.VMEM((2, 1, t), F32),
                        pltpu.VMEM((2, DIFF_VT_ROWS, t), F32)],
        compiler_params=_params("arbitrary", "arbitrary"),
        name="diff_attn",
    )(proj, proj, proj, bias, lam_params, subln_g.reshape(C_V_DIM, 1))


def kernel(x, norm_mix_g, norm_ffn_g, norm_final_g, rel_bias_table, w_in_even, w_fnet, w_out_even,
           w_qkv_odd, lambda_q1, lambda_k1, lambda_q2, lambda_k2, subln_g, w_out_odd, w_ff1, w_ff2):
    batch, seq, d_model = x.shape
    depth = norm_mix_g.shape[0]
    table = rel_bias_table.astype(F32)
    dil_bias = _dilated_bias(table)
    diff_t = min(DIFF_T, seq)
    diff_bias = _diff_bias(table, diff_t)
    c_qk_width = C_HEADS * 2 * HEAD_DIM
    odd_scale = jnp.concatenate([jnp.full((1, c_qk_width), LOG2_E / math.sqrt(HEAD_DIM), F32),
                                 jnp.ones((1, w_qkv_odd.shape[2] - c_qk_width), F32)], axis=1)
    even_scale = jnp.concatenate([jnp.full((1, A_WIDTH), LOG2_E / math.sqrt(HEAD_DIM), F32),
                                  jnp.ones((1, w_in_even.shape[2] - A_WIDTH), F32)], axis=1)

    w_in_even_b, w_qkv_odd_b = w_in_even.astype(BF16), w_qkv_odd.astype(BF16)
    w_ff1_b, w_ff2_b = w_ff1.astype(BF16), w_ff2.astype(BF16)
    w_out_even_f, w_out_odd_f = w_out_even.astype(F32), w_out_odd.astype(F32)

    outs = []
    for bidx in range(batch):
        xs = x.reshape(seq, d_model) if batch == 1 else x[bidx]
        for i in range(depth):
            j = i // 2
            if i % 2 == 0:
                class_dils = tuple(dil for _, dil in DILATED_CONFIGS if dil > 1)
                proj, *by_class = _norm_matmul(xs, norm_mix_g[i], w_in_even_b, j, even_scale,
                                               class_dils=class_dils)
                by_class = dict(zip(class_dils, by_class))
                by_class[1] = proj.reshape(1, *proj.shape)
                branches = [_dilated_branch(by_class[dil], dil_bias[bi])
                            for bi, (_, dil) in enumerate(DILATED_CONFIGS)]
                y, z = _fourier_channel(proj, _fourier_weights(w_fnet[j], seq))
                fb = _fourier_position(y, z)
                mixed = _dilated_merge([b[0] for b in branches], [b[1] for b in branches], fb)
                xs = _matmul_residual(mixed, w_out_even_f, j, xs)
            else:
                lambda_init = 0.8 - 0.6 * math.exp(-0.3 * i)
                proj, = _norm_matmul(xs, norm_mix_g[i], w_qkv_odd_b, j, odd_scale)
                lam_params = jnp.stack([lambda_q1[j], lambda_k1[j], lambda_q2[j], lambda_k2[j]]).astype(F32)
                attn = _diff_attn(proj, diff_bias, lam_params, subln_g[j], lambda_init, t=diff_t)
                xs = _matmul_residual(attn, w_out_odd_f, j, xs)
            xs = _mlp(xs, norm_ffn_g[i], w_ff1_b, w_ff2_b, i, norm_final_g, final_norm=(i == depth - 1))
        outs.append(xs)
    return outs[0].reshape(1, seq, d_model) if batch == 1 else jnp.stack(outs, axis=0)
```

```python
import functools
import math

import numpy as np
import jax
import jax.numpy as jnp
from jax import lax
from jax.experimental import pallas as pl
from jax.experimental.pallas import tpu as pltpu

F32 = jnp.float32
BF16 = jnp.bfloat16

HEAD_DIM = 128
A_HEADS = 12
B_GROUPS = 4
C_HEADS = 8
DILATED_CONFIGS = ((128, 1), (512, 4), (2048, 16))
NUM_BUCKETS = 32
MAX_DISTANCE = 1024
NORM_EPS = 1e-6
NEG_INF = -1e30
LOG2_E = math.log2(math.e)

A_WIDTH = A_HEADS * HEAD_DIM
B_WIDTH = B_GROUPS * HEAD_DIM
C_V_DIM = 2 * HEAD_DIM
HALF_STEPS = 64
assert all(w // (2 * d) == HALF_STEPS for w, d in DILATED_CONFIGS)

LANES = 128
V7X_VMEM_LIMIT_BYTES = 58 * 1024 * 1024

DIL_TQ = 1024
DIL_SUB = 128
DIL_BAND = DIL_SUB + 2 * HALF_STEPS
DIL_HG = 4
DIFF_T = 512
DIFF_TK = 512
DIFF_TILES_PER_STEP = 8
DIFF_R = 3
BF16_SUBLANES = 16
DIFF_VT_ROWS = C_V_DIM + BF16_SUBLANES
FFT_N2 = 128


def _params(*sem, flags=None):
    return pltpu.CompilerParams(dimension_semantics=sem, vmem_limit_bytes=V7X_VMEM_LIMIT_BYTES, flags=flags)


def _magnitude_thresholds():
    half = NUM_BUCKETS // 2
    max_exact = half // 2
    n = np.arange(1, 4 * MAX_DISTANCE, dtype=np.int64)
    large = max_exact + (np.log(n / max_exact) / math.log(MAX_DISTANCE / max_exact)
                         * (half - max_exact)).astype(np.int64)
    large = np.minimum(large, half - 1)
    bucket = np.where(n < max_exact, n, large)
    assert np.all(np.diff(bucket) >= 0)
    thr = {b: int(n[np.argmax(bucket >= b)]) for b in range(1, half)}
    return thr


_THR = _magnitude_thresholds()
_HALF_BUCKETS = NUM_BUCKETS // 2


def _mag_bucket(n):
    return sum(1 for b in range(1, _HALF_BUCKETS) if n >= _THR[b])


def _bias_chain(rel, lo, hi, tab):
    n = jnp.abs(rel)

    def side(base, nlo, nhi):
        bmin, bmax = _mag_bucket(nlo), _mag_bucket(nhi)
        val = jnp.full(rel.shape, tab(base + bmax), F32)
        for b in range(bmax - 1, bmin - 1, -1):
            val = jnp.where(n < _THR[b + 1], tab(base + b), val)
        return val

    if lo > 0:
        return side(_HALF_BUCKETS, lo, hi)
    if hi <= 0:
        return side(0, -hi, -lo)
    return jnp.where(rel > 0, side(_HALF_BUCKETS, 1, hi), side(0, 0, -lo))


def _rms(xf, g):
    ms = jnp.mean(xf * xf, axis=-1, keepdims=True)
    return xf * lax.rsqrt(ms + NORM_EPS) * g


def _dot(a, b):
    return jnp.dot(a, b, preferred_element_type=F32)


def _dot_nt(a, b):
    return lax.dot_general(a, b, (((1,), (1,)), ((), ())), preferred_element_type=F32)


def _norm_matmul_kernel(x_ref, g_ref, w_ref, cs_ref, o_ref, *rest, class_dils):
    class_refs = rest[:len(class_dils)]
    hn_ref = rest[len(class_dils)]

    @pl.when(pl.program_id(1) == 0)
    def _():
        hn_ref[...] = _rms(x_ref[...], g_ref[...]).astype(BF16)

    y = _dot(hn_ref[...], w_ref[...]) * cs_ref[...]
    o_ref[...] = y.astype(o_ref.dtype)

    if class_dils:
        y_sc = rest[len(class_dils) + 1]
        tm, tn = y.shape
        for c in range(tn // LANES):
            y_sc[c] = y[:, LANES * c:LANES * (c + 1)]
        for ref, dil in zip(class_refs, class_dils):
            for r in range(dil):
                for c in range(tn // LANES):
                    ref[r, :, LANES * c:LANES * (c + 1)] = (
                        y_sc[c, pl.ds(r, tm // dil, stride=dil), :].astype(ref.dtype))


def _norm_matmul(x, g, w, layer, col_scale, *, class_dils=(), tm=1024, tn=1024):
    m, d = x.shape
    n = w.shape[2]
    tm = min(tm, m)
    out_specs = [pl.BlockSpec((tm, tn), lambda i, j: (i, j))]
    out_shape = [jax.ShapeDtypeStruct((m, n), BF16)]
    scratch = [pltpu.VMEM((tm, d), BF16)]
    for dil in class_dils:
        out_specs.append(pl.BlockSpec((dil, tm // dil, tn), lambda i, j: (0, i, j)))
        out_shape.append(jax.ShapeDtypeStruct((dil, m // dil, n), BF16))
    if class_dils:
        scratch.append(pltpu.VMEM((tn // LANES, tm, LANES), F32))
    return pl.pallas_call(
        functools.partial(_norm_matmul_kernel, class_dils=tuple(class_dils)),
        grid=(m // tm, n // tn),
        in_specs=[pl.BlockSpec((tm, d), lambda i, j: (i, 0)),
                  pl.BlockSpec((1, d), lambda i, j: (0, 0)),
                  pl.BlockSpec((None, d, tn), lambda i, j: (layer, 0, j)),
                  pl.BlockSpec((1, tn), lambda i, j: (0, j))],
        out_specs=out_specs,
        out_shape=out_shape,
        scratch_shapes=scratch,
        compiler_params=_params("parallel", "arbitrary"),
        name="norm_matmul",
    )(x, g.reshape(1, d), w, col_scale)


def _matmul_residual_kernel(a_ref, w_ref, r_ref, o_ref, wb_ref):
    @pl.when(pl.program_id(0) == 0)
    def _():
        wb_ref[...] = w_ref[...].astype(BF16)

    o_ref[...] = r_ref[...] + _dot(a_ref[...], wb_ref[...])


def _matmul_residual(a, w, layer, res, *, tm=512):
    m, k = a.shape
    n = w.shape[2]
    tm = min(tm, m)
    return pl.pallas_call(
        _matmul_residual_kernel,
        grid=(m // tm,),
        in_specs=[pl.BlockSpec((tm, k), lambda i: (i, 0)),
                  pl.BlockSpec((None, k, n), lambda i: (layer, 0, 0), pipeline_mode=pl.Buffered(1)),
                  pl.BlockSpec((tm, n), lambda i: (i, 0))],
        out_specs=pl.BlockSpec((tm, n), lambda i: (i, 0)),
        out_shape=jax.ShapeDtypeStruct((m, n), F32),
        scratch_shapes=[pltpu.VMEM((k, n), BF16)],
        compiler_params=_params("arbitrary"),
        name="matmul_residual",
    )(a, w, res)


def _mlp_kernel(x_ref, g_ref, w1_ref, w2_ref, gf_ref, o_ref, hn_ref, *, final_norm):
    c = pl.program_id(1)

    @pl.when(c == 0)
    def _():
        x = x_ref[...]
        hn_ref[...] = _rms(x, g_ref[...]).astype(BF16)
        o_ref[...] = x

    u = jnp.maximum(_dot(hn_ref[...], w1_ref[...]), 0.0)
    o_ref[...] += _dot((u * u).astype(BF16), w2_ref[...])

    if final_norm:
        @pl.when(c == pl.num_programs(1) - 1)
        def _():
            o_ref[...] = _rms(o_ref[...], gf_ref[...])


def _mlp(x, g, w1, w2, layer, gf, *, final_norm, tm=1024, tf=1024):
    m, d = x.shape
    f = w1.shape[2]
    tm = min(tm, m)
    return pl.pallas_call(
        functools.partial(_mlp_kernel, final_norm=final_norm),
        grid=(m // tm, f // tf),
        in_specs=[pl.BlockSpec((tm, d), lambda i, c: (i, 0)),
                  pl.BlockSpec((1, d), lambda i, c: (0, 0)),
                  pl.BlockSpec((None, d, tf), lambda i, c: (layer, 0, c)),
                  pl.BlockSpec((None, tf, d), lambda i, c: (layer, c, 0)),
                  pl.BlockSpec((1, d), lambda i, c: (0, 0))],
        out_specs=pl.BlockSpec((tm, d), lambda i, c: (i, 0)),
        out_shape=jax.ShapeDtypeStruct((m, d), F32),
        scratch_shapes=[pltpu.VMEM((tm, d), BF16)],
        compiler_params=_params("parallel", "arbitrary"),
        name="mlp",
    )(x, g.reshape(1, d), w1, w2, gf.reshape(1, d))


def _dilated_bias_kernel(tab_ref, o_ref):
    h = pl.program_id(0)
    row = lax.broadcasted_iota(jnp.int32, (DIL_SUB, DIL_BAND), 0)
    col = lax.broadcasted_iota(jnp.int32, (DIL_SUB, DIL_BAND), 1)
    steps = col - row - HALF_STEPS
    in_band = jnp.abs(steps) <= HALF_STEPS
    for bi, (_, dil) in enumerate(DILATED_CONFIGS):
        reach = HALF_STEPS * dil
        rel = jnp.clip(steps, -HALF_STEPS, HALF_STEPS) * dil
        bias = _bias_chain(rel, -reach, reach, lambda b: tab_ref[b, h] * LOG2_E)
        o_ref[bi, 0] = jnp.where(in_band, bias, NEG_INF)


def _dilated_bias(table):
    nb = len(DILATED_CONFIGS)
    return pl.pallas_call(
        _dilated_bias_kernel,
        grid=(A_HEADS,),
        in_specs=[pl.BlockSpec(memory_space=pltpu.SMEM)],
        out_specs=pl.BlockSpec((nb, 1, DIL_SUB, DIL_BAND), lambda h: (0, h, 0, 0)),
        out_shape=jax.ShapeDtypeStruct((nb, A_HEADS, DIL_SUB, DIL_BAND), F32),
        compiler_params=_params("arbitrary"),
        name="dilated_bias",
    )(table)


def _dilated_branch_kernel(q_ref, kp_ref, kc_ref, kn_ref, vp_ref, vc_ref, vn_ref, b_ref,
                           o_ref, lse_ref, *, tq):
    tb = pl.program_id(2)
    last_tb = pl.num_programs(2) - 1
    nsub = tq // DIL_SUB
    col = lax.broadcasted_iota(jnp.int32, (DIL_SUB, DIL_BAND), 1)
    lane = lax.broadcasted_iota(jnp.int32, (DIL_SUB, LANES), 1)
    lanes_per_head = LANES // DIL_HG

    def band(prev_ref, cur_ref, next_ref, i, hs):
        lo = DIL_SUB * i - HALF_STEPS
        hi = lo + DIL_BAND
        parts = []
        if lo < 0:
            parts.append(prev_ref[:, hs])
        parts.append(cur_ref[max(lo, 0):min(hi, tq), hs])
        if hi > tq:
            parts.append(next_ref[:, hs])
        return parts[0] if len(parts) == 1 else jnp.concatenate(parts, axis=0)

    for i in range(nsub):
        lse_blk = None
        for j in range(DIL_HG):
            hs = slice(HEAD_DIM * j, HEAD_DIM * (j + 1))
            q = q_ref[DIL_SUB * i:DIL_SUB * (i + 1), hs]
            kb = band(kp_ref, kc_ref, kn_ref, i, hs)
            vb = band(vp_ref, vc_ref, vn_ref, i, hs)
            s = _dot_nt(q, kb) + b_ref[j]
            if i == 0:
                s = jnp.where((tb > 0) | (col >= HALF_STEPS), s, NEG_INF)
            if i == nsub - 1:
                s = jnp.where((tb < last_tb) | (col < DIL_BAND - HALF_STEPS), s, NEG_INF)
            m = jnp.max(s, axis=-1, keepdims=True)
            p = jnp.exp2(s - m)
            den = jnp.sum(p, axis=-1, keepdims=True)
            o = _dot(p.astype(BF16), vb) / den
            o_ref[DIL_SUB * i:DIL_SUB * (i + 1), hs] = o.astype(o_ref.dtype)
            lse = m + jnp.log2(den)
            lse_blk = lse if lse_blk is None else jnp.where(lane >= lanes_per_head * j, lse, lse_blk)
        lse_ref[DIL_SUB * i:DIL_SUB * (i + 1), :] = jnp.broadcast_to(lse_blk, (DIL_SUB, LANES))


def _dilated_branch(proj_c, bias_b, *, tq=DIL_TQ):
    dil, l, w = proj_c.shape
    tq = min(tq, l)
    ngroups = A_HEADS // DIL_HG
    gw = DIL_HG * HEAD_DIM
    qoff, koff, voff = 0, A_WIDTH // gw, 2 * A_WIDTH // gw
    halo_per_tq = tq // HALF_STEPS
    n_halo = l // HALF_STEPS

    def cur(off):
        return pl.BlockSpec((None, tq, gw), lambda r, g, t: (r, t, off + g))

    def prev(off):
        return pl.BlockSpec((None, HALF_STEPS, gw),
                            lambda r, g, t: (r, jnp.maximum(t * halo_per_tq - 1, 0), off + g))

    def nxt(off):
        return pl.BlockSpec((None, HALF_STEPS, gw),
                            lambda r, g, t: (r, jnp.minimum((t + 1) * halo_per_tq, n_halo - 1), off + g))

    return pl.pallas_call(
        functools.partial(_dilated_branch_kernel, tq=tq),
        grid=(dil, ngroups, l // tq),
        in_specs=[cur(qoff), prev(koff), cur(koff), nxt(koff), prev(voff), cur(voff), nxt(voff),
                  pl.BlockSpec((DIL_HG, DIL_SUB, DIL_BAND), lambda r, g, t: (g, 0, 0))],
        out_specs=[pl.BlockSpec((None, tq, gw), lambda r, g, t: (r, t, g)),
                   pl.BlockSpec((None, tq, LANES), lambda r, g, t: (r, t, g))],
        out_shape=[jax.ShapeDtypeStruct((dil, l, A_WIDTH), BF16),
                   jax.ShapeDtypeStruct((dil, l, ngroups * LANES), F32)],
        compiler_params=_params("parallel", "parallel", "arbitrary"),
        name=f"dilated_branch_d{dil}",
    )(*([proj_c] * 7), bias_b)


def _dilated_merge_kernel(*refs, dils):
    nb = len(dils)
    o_refs, l_refs = refs[:nb], refs[nb:2 * nb]
    fb_ref, out_ref, l_sc, o_sc = refs[2 * nb:]
    g = pl.program_id(1)
    ngroups = A_HEADS // DIL_HG
    lanes_per_head = LANES // DIL_HG
    tm = out_ref.shape[0]

    @pl.when(g < ngroups)
    def _():
        for b, dil in enumerate(dils):
            for r in range(dil):
                rows = pl.ds(r, tm // dil, stride=dil)
                l_sc[b, rows, :] = l_refs[b][r]
                for j in range(DIL_HG):
                    o_sc[b, j, rows, :] = o_refs[b][r, :, HEAD_DIM * j:HEAD_DIM * (j + 1)].astype(F32)
        ls = [l_sc[b] for b in range(nb)]
        mx = functools.reduce(jnp.maximum, ls)
        es = [jnp.exp2(l - mx) for l in ls]
        inv = 1.0 / functools.reduce(jnp.add, es)
        for j in range(DIL_HG):
            lc = slice(lanes_per_head * j, lanes_per_head * j + 1)
            acc = functools.reduce(jnp.add, [(es[b] * inv)[:, lc] * o_sc[b, j] for b in range(nb)])
            out_ref[:, HEAD_DIM * j:HEAD_DIM * (j + 1)] = acc.astype(out_ref.dtype)

    @pl.when(g == ngroups)
    def _():
        out_ref[...] = fb_ref[...]


def _dilated_merge(os_, lses, fb, *, tm=1024):
    s = fb.shape[0]
    tm = min(tm, s)
    ngroups = A_HEADS // DIL_HG
    gw = DIL_HG * HEAD_DIM
    assert fb.shape[1] == gw and HEAD_DIM == LANES
    dils = tuple(o.shape[0] for o in os_)
    clamp = lambda i, g: (0, i, jnp.minimum(g, ngroups - 1))
    return pl.pallas_call(
        functools.partial(_dilated_merge_kernel, dils=dils),
        grid=(s // tm, ngroups + 1),
        in_specs=[pl.BlockSpec((d, tm // d, gw), clamp) for d in dils]
                 + [pl.BlockSpec((d, tm // d, LANES), clamp) for d in dils]
                 + [pl.BlockSpec((tm, gw), lambda i, g: (i, 0))],
        out_specs=pl.BlockSpec((tm, gw), lambda i, g: (i, g)),
        out_shape=jax.ShapeDtypeStruct((s, A_WIDTH + B_WIDTH), BF16),
        scratch_shapes=[pltpu.VMEM((len(dils), tm, LANES), F32),
                        pltpu.VMEM((len(dils), DIL_HG, tm, LANES), F32)],
        compiler_params=_params("parallel", "arbitrary"),
        name="dilated_merge",
    )(*os_, *lses, fb)


def _dft_cos_sin(n):
    idx = np.arange(n, dtype=np.int64)
    ang = 2.0 * np.pi * ((idx[:, None] * idx[None, :]) % n) / n
    return np.cos(ang), np.sin(ang)


def _fourier_weights_kernel(cs_ref, w_ref, o_ref, *, norm):
    w = w_ref[0]
    ab = jnp.dot(cs_ref[...], w, preferred_element_type=F32, precision=lax.Precision.HIGHEST) * norm
    o_ref[0] = jnp.concatenate([ab[:HEAD_DIM], ab[HEAD_DIM:]], axis=1).astype(o_ref.dtype)


def _fourier_weights(w_f, seq):
    c, s = _dft_cos_sin(HEAD_DIM)
    cs = jnp.asarray(np.concatenate([c, s], axis=0), F32)
    norm = 1.0 / math.sqrt(seq * HEAD_DIM)
    return pl.pallas_call(
        functools.partial(_fourier_weights_kernel, norm=norm),
        grid=(B_GROUPS,),
        in_specs=[pl.BlockSpec((2 * HEAD_DIM, HEAD_DIM), lambda g: (0, 0)),
                  pl.BlockSpec((1, HEAD_DIM, HEAD_DIM), lambda g: (g, 0, 0))],
        out_specs=pl.BlockSpec((1, HEAD_DIM, 2 * HEAD_DIM), lambda g: (g, 0, 0)),
        out_shape=jax.ShapeDtypeStruct((B_GROUPS, HEAD_DIM, 2 * HEAD_DIM), BF16),
        compiler_params=_params("arbitrary"),
        name="fourier_weights",
    )(cs, w_f)


def _fourier_channel_kernel(u_ref, ab_ref, y_ref, z_ref):
    for g in range(B_GROUPS):
        hs = slice(HEAD_DIM * g, HEAD_DIM * (g + 1))
        yz = _dot(u_ref[:, hs], ab_ref[g])
        y_ref[:, hs] = yz[:, :HEAD_DIM].astype(y_ref.dtype)
        z_ref[:, hs] = yz[:, HEAD_DIM:].astype(z_ref.dtype)


def _fourier_channel(proj, ab, *, tm=1024):
    s, w = proj.shape
    ublock = (w - B_WIDTH) // B_WIDTH
    assert ublock * B_WIDTH == w - B_WIDTH
    spec = pl.BlockSpec((tm, B_WIDTH), lambda i: (i, 0))
    return pl.pallas_call(
        _fourier_channel_kernel,
        grid=(s // tm,),
        in_specs=[pl.BlockSpec((tm, B_WIDTH), lambda i: (i, ublock)),
                  pl.BlockSpec((B_GROUPS, HEAD_DIM, 2 * HEAD_DIM), lambda i: (0, 0, 0))],
        out_specs=[spec, spec],
        out_shape=[jax.ShapeDtypeStruct((s, B_WIDTH), BF16)] * 2,
        compiler_params=_params("parallel"),
        name="fourier_channel",
    )(proj, ab)


def _fourier_stage1_kernel(m1_ref, tc_ref, ts_ref, y_ref, z_ref, tre_ref, tim_ref, *, n1, n2_per_step):
    yz = jnp.concatenate([y_ref[...], z_ref[...]], axis=0)
    ab = _dot(m1_ref[...], yz)
    a, b = ab[:n1], ab[n1:]
    ch = B_WIDTH
    for q in range(n2_per_step):
        cs = slice(ch * q, ch * (q + 1))
        c = tc_ref[0, :, q:q + 1]
        s = ts_ref[0, :, q:q + 1]
        aq, bq = a[:, cs], b[:, cs]
        tre_ref[:, cs] = (aq * c + bq * s).astype(tre_ref.dtype)
        tim_ref[:, cs] = (bq * c - aq * s).astype(tim_ref.dtype)


def _fourier_stage2_kernel(m2_ref, tre_ref, tim_ref, o_ref, *, n2, k1_per_step):
    ch = B_WIDTH
    for q in range(k1_per_step):
        rs = slice(n2 * q, n2 * (q + 1))
        t = jnp.concatenate([tre_ref[rs, :], tim_ref[rs, :]], axis=0)
        o_ref[:, ch * q:ch * (q + 1)] = _dot(m2_ref[...], t).astype(o_ref.dtype)


def _fourier_position(y, z, *, n2=FFT_N2, n2_per_step=16, k1_per_step=4):
    seq, ch = y.shape
    n1 = seq // n2
    n2_per_step = min(n2_per_step, n2)
    k1_per_step = min(k1_per_step, n1)
    c1, s1 = _dft_cos_sin(n1)
    m1 = jnp.asarray(np.block([[c1, -s1], [-s1, -c1]]), BF16)
    c2, s2 = _dft_cos_sin(n2)
    m2 = jnp.asarray(np.concatenate([c2, s2], axis=1), BF16)
    k1 = np.arange(n1, dtype=np.int64)[:, None]
    nn2 = np.arange(n2, dtype=np.int64)[None, :]
    ang = 2.0 * np.pi * ((k1 * nn2) % seq) / seq
    steps = n2 // n2_per_step
    tc = jnp.asarray(np.cos(ang).reshape(n1, steps, n2_per_step).transpose(1, 0, 2), F32)
    ts = jnp.asarray(np.sin(ang).reshape(n1, steps, n2_per_step).transpose(1, 0, 2), F32)

    cols = n2_per_step * ch
    dspec = pl.BlockSpec((n1, cols), lambda t: (0, t))
    tspec = pl.BlockSpec((1, n1, n2_per_step), lambda t: (t, 0, 0))
    tre, tim = pl.pallas_call(
        functools.partial(_fourier_stage1_kernel, n1=n1, n2_per_step=n2_per_step),
        grid=(steps,),
        in_specs=[pl.BlockSpec((2 * n1, 2 * n1), lambda t: (0, 0)), tspec, tspec, dspec, dspec],
        out_specs=[dspec, dspec],
        out_shape=[jax.ShapeDtypeStruct((n1, n2 * ch), BF16)] * 2,
        compiler_params=_params("parallel"),
        name="fourier_stage1",
    )(m1, tc, ts, y.reshape(n1, n2 * ch), z.reshape(n1, n2 * ch))

    tblock = pl.BlockSpec((k1_per_step * n2, ch), lambda t: (t, 0))
    out = pl.pallas_call(
        functools.partial(_fourier_stage2_kernel, n2=n2, k1_per_step=k1_per_step),
        grid=(n1 // k1_per_step,),
        in_specs=[pl.BlockSpec((n2, 2 * n2), lambda t: (0, 0)), tblock, tblock],
        out_specs=pl.BlockSpec((n2, k1_per_step * ch), lambda t: (0, t)),
        out_shape=jax.ShapeDtypeStruct((n2, n1 * ch), BF16),
        compiler_params=_params("parallel"),
        name="fourier_stage2",
    )(m2, tre.reshape(seq, ch), tim.reshape(seq, ch))
    return out.reshape(seq, ch)


def _diff_bias_kernel(tab_ref, o_ref, *, t):
    h = pl.program_id(0)
    row = lax.broadcasted_iota(jnp.int32, (t, t), 0)
    col = lax.broadcasted_iota(jnp.int32, (t, t), 1)
    base = row - col
    for r in range(2 * DIFF_R + 1):
        d = (r - DIFF_R) * t
        o_ref[0, r] = _bias_chain(base + d, d - (t - 1), d + (t - 1),
                                  lambda b: tab_ref[b, A_HEADS + h] * LOG2_E)


def _diff_bias(table, t):
    assert DIFF_R * t - (t - 1) >= _THR[_HALF_BUCKETS - 1]
    nt = 2 * DIFF_R + 1
    return pl.pallas_call(
        functools.partial(_diff_bias_kernel, t=t),
        grid=(C_HEADS,),
        in_specs=[pl.BlockSpec(memory_space=pltpu.SMEM)],
        out_specs=pl.BlockSpec((1, nt, t, t), lambda h: (h, 0, 0, 0)),
        out_shape=jax.ShapeDtypeStruct((C_HEADS, nt, t, t), F32),
        compiler_params=_params("arbitrary"),
        name="diff_bias",
    )(table)


def _diff_attn_kernel(q_ref, k_ref, v_ref, b_ref, lam_ref, g_ref, o_ref,
                      vt_sc, s0_sc, s1_sc, p0_sc, p1_sc, alpha0_sc, alpha1_sc, mcur0_sc, mcur1_sc, m_sc, acc_sc,
                      *, t, tk, lambda_init):
    s_sc, p_sc, alpha_sc = (s0_sc, s1_sc), (p0_sc, p1_sc), (alpha0_sc, alpha1_sc)
    mcur_sc = (mcur0_sc, mcur1_sc)
    qg = pl.program_id(1)
    nkv = k_ref.shape[0] // tk
    ntile = q_ref.shape[0] // t
    total = ntile * nkv
    per_tile = t // tk

    @pl.when(qg == 0)
    def _():
        pad = lax.broadcasted_iota(jnp.int32, (DIFF_VT_ROWS - C_V_DIM, tk), 0)
        ones_row = jnp.where(pad == 0, 1.0, 0.0).astype(BF16)
        for j in range(nkv):
            vt_sc[j, :C_V_DIM, :] = v_ref[tk * j:tk * (j + 1), :].astype(F32).T.astype(BF16)
            vt_sc[j, C_V_DIM:, :] = ones_row

    def split(n):
        return n // nkv, n % nkv

    def key_rows(j):
        return pl.ds(pl.multiple_of(j * tk, tk), tk)

    def query_rows(tile):
        return pl.ds(pl.multiple_of(tile * t, t), t)

    def logits(n, buf):
        tile, j = split(n)
        q_tile = qg * ntile + tile
        for c in range(2):
            hs = slice(HEAD_DIM * c, HEAD_DIM * (c + 1))
            q = q_ref[query_rows(tile), hs]
            if tk <= t:
                bias_tile = jnp.clip(j // per_tile - q_tile, -DIFF_R, DIFF_R) + DIFF_R
                bias = b_ref[0, bias_tile, pl.ds(pl.multiple_of((j % per_tile) * tk, tk), tk), :]
                logit = _dot_nt(k_ref[key_rows(j), hs], q) + bias
                s_sc[buf][c] = logit
                mcur_sc[buf][c] = jnp.max(logit, axis=0, keepdims=True)
            else:
                for r in range(tk // t):
                    bias_tile = jnp.clip(j * (tk // t) + r - q_tile, -DIFF_R, DIFF_R) + DIFF_R
                    k_rows = pl.ds(pl.multiple_of(j * tk + r * t, t), t)
                    s_sc[buf][c, t * r:t * (r + 1), :] = _dot_nt(k_ref[k_rows, hs], q) + b_ref[0, bias_tile]

    def softmax(n, buf):
        _, j = split(n)
        for c in range(2):
            m_cur = mcur_sc[buf][c] if tk <= t else jnp.max(s_sc[buf][c], axis=0, keepdims=True)
            m_prev = jnp.where(j == 0, NEG_INF, m_sc[c])
            m_next = jnp.maximum(m_prev, m_cur)
            alpha_sc[buf][c] = jnp.exp2(m_prev - m_next)
            m_sc[c] = m_next
            p_sc[buf][c] = jnp.exp2(s_sc[buf][c] - m_next).astype(BF16)

    def values(n, buf):
        _, j = split(n)
        for c in range(2):
            acc_sc[c] = acc_sc[c] * alpha_sc[buf][c] + _dot(vt_sc[j], p_sc[buf][c])

    def finalize(tile):
        lp = lam_ref[...]
        lam = (jnp.exp(jnp.sum(lp[0:1] * lp[1:2], axis=-1, keepdims=True))
               - jnp.exp(jnp.sum(lp[2:3] * lp[3:4], axis=-1, keepdims=True)) + lambda_init)
        num0, den0 = acc_sc[0, :C_V_DIM, :], acc_sc[0, C_V_DIM:C_V_DIM + 1, :]
        num1, den1 = acc_sc[1, :C_V_DIM, :], acc_sc[1, C_V_DIM:C_V_DIM + 1, :]
        o = num0 * (1.0 / den0) - lam * (num1 * (1.0 / den1))
        ms = jnp.mean(o * o, axis=0, keepdims=True)
        o = o * lax.rsqrt(ms + NORM_EPS) * (g_ref[...] * (1.0 - lambda_init))
        o_ref[query_rows(tile), :] = o.T.astype(o_ref.dtype)

    m_sc[...] = jnp.full(m_sc.shape, NEG_INF, F32)
    acc_sc[...] = jnp.zeros(acc_sc.shape, F32)

    logits(0, 0)
    logits(1, 1)
    softmax(0, 0)

    def body(i, carry):
        n = 2 * i + 1
        logits(n + 1, 0)
        softmax(n, 1)
        values(n - 1, 0)
        logits(n + 2, 1)
        softmax(n + 1, 0)
        values(n, 1)

        @pl.when((n + 1) % nkv == 0)
        def _():
            finalize(n // nkv)

        return carry

    lax.fori_loop(0, total // 2 - 1, body, 0)
    softmax(total - 1, 1)
    values(total - 2, 0)
    values(total - 1, 1)
    finalize(ntile - 1)


def _diff_attn(proj, bias, lam_params, subln_g, lambda_init, *, t, tk=DIFF_TK, tiles_per_step=DIFF_TILES_PER_STEP):
    s = proj.shape[0]
    nt = 2 * DIFF_R + 1
    tk = min(tk, s // 4)
    nkv = s // tk
    tiles_per_step = min(tiles_per_step, s // t)
    tq = tiles_per_step * t
    assert nkv * tk == s and nkv % 2 == 0 and nkv >= 4 and (t % tk == 0 or tk % t == 0) and s % tq == 0
    return pl.pallas_call(
        functools.partial(_diff_attn_kernel, t=t, tk=tk, lambda_init=lambda_init),
        grid=(C_HEADS, s // tq),
        in_specs=[pl.BlockSpec((tq, C_V_DIM), lambda h, i: (i, h)),
                  pl.BlockSpec((s, C_V_DIM), lambda h, i: (0, C_HEADS + h)),
                  pl.BlockSpec((s, C_V_DIM), lambda h, i: (0, 2 * C_HEADS + h)),
                  pl.BlockSpec((1, nt, t, t), lambda h, i: (h, 0, 0, 0)),
                  pl.BlockSpec((4, HEAD_DIM), lambda h, i: (0, 0)),
                  pl.BlockSpec((C_V_DIM, 1), lambda h, i: (0, 0))],
        out_specs=pl.BlockSpec((tq, C_V_DIM), lambda h, i: (i, h)),
        out_shape=jax.ShapeDtypeStruct((s, C_HEADS * C_V_DIM), BF16),
        scratch_shapes=[pltpu.VMEM((nkv, DIFF_VT_ROWS, tk), BF16),
                        pltpu.VMEM((2, tk, t), F32), pltpu.VMEM((2, tk, t), F32),
                        pltpu.VMEM((2, tk, t), BF16), pltpu.VMEM((2, tk, t), BF16),
                        pltpu.VMEM((2, 1, t), F32), pltpu.VMEM((2, 1, t), F32),
                        pltpu.VMEM((2, 1, t), F32), pltpu.VMEM((2, 1, t), F32), pltpu.VMEM((2, 1, t), F32),
                        pltpu.VMEM((2, DIFF_VT_ROWS, t), F32)],
        compiler_params=_params("arbitrary", "arbitrary"),
        name="diff_attn",
    )(proj, proj, proj, bias, lam_params, subln_g.reshape(C_V_DIM, 1))


def kernel(x, norm_mix_g, norm_ffn_g, norm_final_g, rel_bias_table, w_in_even, w_fnet, w_out_even,
           w_qkv_odd, lambda_q1, lambda_k1, lambda_q2, lambda_k2, subln_g, w_out_odd, w_ff1, w_ff2):
    batch, seq, d_model = x.shape
    depth = norm_mix_g.shape[0]
    table = rel_bias_table.astype(F32)
    dil_bias = _dilated_bias(table)
    diff_t = min(DIFF_T, seq)
    diff_bias = _diff_bias(table, diff_t)
    c_qk_width = C_HEADS * 2 * HEAD_DIM
    odd_scale = jnp.concatenate([jnp.full((1, c_qk_width), LOG2_E / math.sqrt(HEAD_DIM), F32),
                                 jnp.ones((1, w_qkv_odd.shape[2] - c_qk_width), F32)], axis=1)
    even_scale = jnp.concatenate([jnp.full((1, A_WIDTH), LOG2_E / math.sqrt(HEAD_DIM), F32),
                                  jnp.ones((1, w_in_even.shape[2] - A_WIDTH), F32)], axis=1)

    w_in_even_b, w_qkv_odd_b = w_in_even.astype(BF16), w_qkv_odd.astype(BF16)
    w_ff1_b, w_ff2_b = w_ff1.astype(BF16), w_ff2.astype(BF16)
    w_out_even_f, w_out_odd_f = w_out_even.astype(F32), w_out_odd.astype(F32)

    outs = []
    for bidx in range(batch):
        xs = x.reshape(seq, d_model) if batch == 1 else x[bidx]
        for i in range(depth):
            j = i // 2
            if i % 2 == 0:
                class_dils = tuple(dil for _, dil in DILATED_CONFIGS if dil > 1)
                proj, *by_class = _norm_matmul(xs, norm_mix_g[i], w_in_even_b, j, even_scale,
                                               class_dils=class_dils)
                by_class = dict(zip(class_dils, by_class))
                by_class[1] = proj.reshape(1, *proj.shape)
                branches = [_dilated_branch(by_class[dil], dil_bias[bi])
                            for bi, (_, dil) in enumerate(DILATED_CONFIGS)]
                y, z = _fourier_channel(proj, _fourier_weights(w_fnet[j], seq))
                fb = _fourier_position(y, z)
                mixed = _dilated_merge([b[0] for b in branches], [b[1] for b in branches], fb)
                xs = _matmul_residual(mixed, w_out_even_f, j, xs)
            else:
                lambda_init = 0.8 - 0.6 * math.exp(-0.3 * i)
                proj, = _norm_matmul(xs, norm_mix_g[i], w_qkv_odd_b, j, odd_scale)
                lam_params = jnp.stack([lambda_q1[j], lambda_k1[j], lambda_q2[j], lambda_k2[j]]).astype(F32)
                attn = _diff_attn(proj, diff_bias, lam_params, subln_g[j], lambda_init, t=diff_t)
                xs = _matmul_residual(attn, w_out_odd_f, j, xs)
            xs = _mlp(xs, norm_ffn_g[i], w_ff1_b, w_ff2_b, i, norm_final_g, final_norm=(i == depth - 1))
        outs.append(xs)
    return outs[0].reshape(1, seq, d_model) if batch == 1 else jnp.stack(outs, axis=0)
```

```python
import functools
import math

import numpy as np
import jax
import jax.numpy as jnp
from jax import lax
from jax.experimental import pallas as pl
from jax.experimental.pallas import tpu as pltpu

F32 = jnp.float32
BF16 = jnp.bfloat16

HEAD_DIM = 128
A_HEADS = 12
B_GROUPS = 4
C_HEADS = 8
DILATED_CONFIGS = ((128, 1), (512, 4), (2048, 16))
NUM_BUCKETS = 32
MAX_DISTANCE = 1024
NORM_EPS = 1e-6
NEG_INF = -1e30
LOG2_E = math.log2(math.e)

A_WIDTH = A_HEADS * HEAD_DIM
B_WIDTH = B_GROUPS * HEAD_DIM
C_V_DIM = 2 * HEAD_DIM
HALF_STEPS = 64
assert all(w // (2 * d) == HALF_STEPS for w, d in DILATED_CONFIGS)

LANES = 128
V7X_VMEM_LIMIT_BYTES = 58 * 1024 * 1024

DIL_TQ = 1024
DIL_SUB = 128
DIL_BAND = DIL_SUB + 2 * HALF_STEPS
DIL_HG = 4
DIFF_T = 512
DIFF_TK = 512
DIFF_TILES_PER_STEP = 8
DIFF_R = 3
BF16_SUBLANES = 16
DIFF_VT_ROWS = C_V_DIM + BF16_SUBLANES
FFT_N2 = 128


def _params(*sem, flags=None):
    return pltpu.CompilerParams(dimension_semantics=sem, vmem_limit_bytes=V7X_VMEM_LIMIT_BYTES, flags=flags)


def _magnitude_thresholds():
    half = NUM_BUCKETS // 2
    max_exact = half // 2
    n = np.arange(1, 4 * MAX_DISTANCE, dtype=np.int64)
    large = max_exact + (np.log(n / max_exact) / math.log(MAX_DISTANCE / max_exact)
                         * (half - max_exact)).astype(np.int64)
    large = np.minimum(large, half - 1)
    bucket = np.where(n < max_exact, n, large)
    assert np.all(np.diff(bucket) >= 0)
    thr = {b: int(n[np.argmax(bucket >= b)]) for b in range(1, half)}
    return thr


_THR = _magnitude_thresholds()
_HALF_BUCKETS = NUM_BUCKETS // 2


def _mag_bucket(n):
    return sum(1 for b in range(1, _HALF_BUCKETS) if n >= _THR[b])


def _bias_chain(rel, lo, hi, tab):
    n = jnp.abs(rel)

    def side(base, nlo, nhi):
        bmin, bmax = _mag_bucket(nlo), _mag_bucket(nhi)
        val = jnp.full(rel.shape, tab(base + bmax), F32)
        for b in range(bmax - 1, bmin - 1, -1):
            val = jnp.where(n < _THR[b + 1], tab(base + b), val)
        return val

    if lo > 0:
        return side(_HALF_BUCKETS, lo, hi)
    if hi <= 0:
        return side(0, -hi, -lo)
    return jnp.where(rel > 0, side(_HALF_BUCKETS, 1, hi), side(0, 0, -lo))


def _rms(xf, g):
    ms = jnp.mean(xf * xf, axis=-1, keepdims=True)
    return xf * lax.rsqrt(ms + NORM_EPS) * g


def _dot(a, b):
    return jnp.dot(a, b, preferred_element_type=F32)


def _dot_nt(a, b):
    return lax.dot_general(a, b, (((1,), (1,)), ((), ())), preferred_element_type=F32)


def _norm_matmul_kernel(x_ref, g_ref, w_ref, cs_ref, o_ref, *rest, class_dils):
    class_refs = rest[:len(class_dils)]
    hn_ref = rest[len(class_dils)]

    @pl.when(pl.program_id(1) == 0)
    def _():
        hn_ref[...] = _rms(x_ref[...], g_ref[...]).astype(BF16)

    y = _dot(hn_ref[...], w_ref[...]) * cs_ref[...]
    o_ref[...] = y.astype(o_ref.dtype)

    if class_dils:
        y_sc = rest[len(class_dils) + 1]
        tm, tn = y.shape
        for c in range(tn // LANES):
            y_sc[c] = y[:, LANES * c:LANES * (c + 1)]
        for ref, dil in zip(class_refs, class_dils):
            for r in range(dil):
                for c in range(tn // LANES):
                    ref[r, :, LANES * c:LANES * (c + 1)] = (
                        y_sc[c, pl.ds(r, tm // dil, stride=dil), :].astype(ref.dtype))


def _norm_matmul(x, g, w, layer, col_scale, *, class_dils=(), tm=1024, tn=1024):
    m, d = x.shape
    n = w.shape[2]
    tm = min(tm, m)
    out_specs = [pl.BlockSpec((tm, tn), lambda i, j: (i, j))]
    out_shape = [jax.ShapeDtypeStruct((m, n), BF16)]
    scratch = [pltpu.VMEM((tm, d), BF16)]
    for dil in class_dils:
        out_specs.append(pl.BlockSpec((dil, tm // dil, tn), lambda i, j: (0, i, j)))
        out_shape.append(jax.ShapeDtypeStruct((dil, m // dil, n), BF16))
    if class_dils:
        scratch.append(pltpu.VMEM((tn // LANES, tm, LANES), F32))
    return pl.pallas_call(
        functools.partial(_norm_matmul_kernel, class_dils=tuple(class_dils)),
        grid=(m // tm, n // tn),
        in_specs=[pl.BlockSpec((tm, d), lambda i, j: (i, 0)),
                  pl.BlockSpec((1, d), lambda i, j: (0, 0)),
                  pl.BlockSpec((None, d, tn), lambda i, j: (layer, 0, j)),
                  pl.BlockSpec((1, tn), lambda i, j: (0, j))],
        out_specs=out_specs,
        out_shape=out_shape,
        scratch_shapes=scratch,
        compiler_params=_params("parallel", "arbitrary"),
        name="norm_matmul",
    )(x, g.reshape(1, d), w, col_scale)


def _matmul_residual_kernel(a_ref, w_ref, r_ref, o_ref, wb_ref):
    @pl.when(pl.program_id(0) == 0)
    def _():
        wb_ref[...] = w_ref[...].astype(BF16)

    o_ref[...] = r_ref[...] + _dot(a_ref[...], wb_ref[...])


def _matmul_residual(a, w, layer, res, *, tm=512):
    m, k = a.shape
    n = w.shape[2]
    tm = min(tm, m)
    return pl.pallas_call(
        _matmul_residual_kernel,
        grid=(m // tm,),
        in_specs=[pl.BlockSpec((tm, k), lambda i: (i, 0)),
                  pl.BlockSpec((None, k, n), lambda i: (layer, 0, 0), pipeline_mode=pl.Buffered(1)),
                  pl.BlockSpec((tm, n), lambda i: (i, 0))],
        out_specs=pl.BlockSpec((tm, n), lambda i: (i, 0)),
        out_shape=jax.ShapeDtypeStruct((m, n), F32),
        scratch_shapes=[pltpu.VMEM((k, n), BF16)],
        compiler_params=_params("arbitrary"),
        name="matmul_residual",
    )(a, w, res)


def _mlp_kernel(x_ref, g_ref, w1_ref, w2_ref, gf_ref, o_ref, hn_ref, *, final_norm):
    c = pl.program_id(1)

    @pl.when(c == 0)
    def _():
        x = x_ref[...]
        hn_ref[...] = _rms(x, g_ref[...]).astype(BF16)
        o_ref[...] = x

    u = jnp.maximum(_dot(hn_ref[...], w1_ref[...]), 0.0)
    o_ref[...] += _dot((u * u).astype(BF16), w2_ref[...])

    if final_norm:
        @pl.when(c == pl.num_programs(1) - 1)
        def _():
            o_ref[...] = _rms(o_ref[...], gf_ref[...])


def _mlp(x, g, w1, w2, layer, gf, *, final_norm, tm=1024, tf=1024):
    m, d = x.shape
    f = w1.shape[2]
    tm = min(tm, m)
    return pl.pallas_call(
        functools.partial(_mlp_kernel, final_norm=final_norm),
        grid=(m // tm, f // tf),
        in_specs=[pl.BlockSpec((tm, d), lambda i, c: (i, 0)),
                  pl.BlockSpec((1, d), lambda i, c: (0, 0)),
                  pl.BlockSpec((None, d, tf), lambda i, c: (layer, 0, c)),
                  pl.BlockSpec((None, tf, d), lambda i, c: (layer, c, 0)),
                  pl.BlockSpec((1, d), lambda i, c: (0, 0))],
        out_specs=pl.BlockSpec((tm, d), lambda i, c: (i, 0)),
        out_shape=jax.ShapeDtypeStruct((m, d), F32),
        scratch_shapes=[pltpu.VMEM((tm, d), BF16)],
        compiler_params=_params("parallel", "arbitrary"),
        name="mlp",
    )(x, g.reshape(1, d), w1, w2, gf.reshape(1, d))


def _dilated_bias_kernel(tab_ref, o_ref):
    h = pl.program_id(0)
    row = lax.broadcasted_iota(jnp.int32, (DIL_SUB, DIL_BAND), 0)
    col = lax.broadcasted_iota(jnp.int32, (DIL_SUB, DIL_BAND), 1)
    steps = col - row - HALF_STEPS
    in_band = jnp.abs(steps) <= HALF_STEPS
    for bi, (_, dil) in enumerate(DILATED_CONFIGS):
        reach = HALF_STEPS * dil
        rel = jnp.clip(steps, -HALF_STEPS, HALF_STEPS) * dil
        bias = _bias_chain(rel, -reach, reach, lambda b: tab_ref[b, h] * LOG2_E)
        o_ref[bi, 0] = jnp.where(in_band, bias, NEG_INF)


def _dilated_bias(table):
    nb = len(DILATED_CONFIGS)
    return pl.pallas_call(
        _dilated_bias_kernel,
        grid=(A_HEADS,),
        in_specs=[pl.BlockSpec(memory_space=pltpu.SMEM)],
        out_specs=pl.BlockSpec((nb, 1, DIL_SUB, DIL_BAND), lambda h: (0, h, 0, 0)),
        out_shape=jax.ShapeDtypeStruct((nb, A_HEADS, DIL_SUB, DIL_BAND), F32),
        compiler_params=_params("arbitrary"),
        name="dilated_bias",
    )(table)


def _dilated_branch_kernel(q_ref, kp_ref, kc_ref, kn_ref, vp_ref, vc_ref, vn_ref, b_ref,
                           o_ref, lse_ref, *, tq):
    tb = pl.program_id(2)
    last_tb = pl.num_programs(2) - 1
    nsub = tq // DIL_SUB
    col = lax.broadcasted_iota(jnp.int32, (DIL_SUB, DIL_BAND), 1)
    lane = lax.broadcasted_iota(jnp.int32, (DIL_SUB, LANES), 1)
    lanes_per_head = LANES // DIL_HG

    def band(prev_ref, cur_ref, next_ref, i, hs):
        lo = DIL_SUB * i - HALF_STEPS
        hi = lo + DIL_BAND
        parts = []
        if lo < 0:
            parts.append(prev_ref[:, hs])
        parts.append(cur_ref[max(lo, 0):min(hi, tq), hs])
        if hi > tq:
            parts.append(next_ref[:, hs])
        return parts[0] if len(parts) == 1 else jnp.concatenate(parts, axis=0)

    for cl in range(q_ref.shape[0]):
        for i in range(nsub):
            lse_blk = None
            for j in range(DIL_HG):
                hs = slice(HEAD_DIM * j, HEAD_DIM * (j + 1))
                q = q_ref[cl, DIL_SUB * i:DIL_SUB * (i + 1), hs]
                kb = band(kp_ref.at[cl], kc_ref.at[cl], kn_ref.at[cl], i, hs)
                vb = band(vp_ref.at[cl], vc_ref.at[cl], vn_ref.at[cl], i, hs)
                s = _dot_nt(q, kb) + b_ref[j]
                if i == 0:
                    s = jnp.where((tb > 0) | (col >= HALF_STEPS), s, NEG_INF)
                if i == nsub - 1:
                    s = jnp.where((tb < last_tb) | (col < DIL_BAND - HALF_STEPS), s, NEG_INF)
                m = jnp.max(s, axis=-1, keepdims=True)
                p = jnp.exp2(s - m)
                den = jnp.sum(p, axis=-1, keepdims=True)
                o = _dot(p.astype(BF16), vb) / den
                o_ref[cl, DIL_SUB * i:DIL_SUB * (i + 1), hs] = o.astype(o_ref.dtype)
                lse = m + jnp.log2(den)
                lse_blk = lse if lse_blk is None else jnp.where(lane >= lanes_per_head * j, lse, lse_blk)
            lse_ref[cl, DIL_SUB * i:DIL_SUB * (i + 1), :] = jnp.broadcast_to(lse_blk, (DIL_SUB, LANES))


def _dilated_branch(proj_c, bias_b, *, tq=DIL_TQ):
    dil, l, w = proj_c.shape
    tq = min(tq, l)
    ngroups = A_HEADS // DIL_HG
    gw = DIL_HG * HEAD_DIM
    qoff, koff, voff = 0, A_WIDTH // gw, 2 * A_WIDTH // gw
    halo_per_tq = tq // HALF_STEPS
    n_halo = l // HALF_STEPS
    cps = max(1, min(dil, DIL_TQ // tq))
    assert dil % cps == 0

    def cur(off):
        return pl.BlockSpec((cps, tq, gw), lambda r, g, t: (r, t, off + g))

    def prev(off):
        return pl.BlockSpec((cps, HALF_STEPS, gw),
                            lambda r, g, t: (r, jnp.maximum(t * halo_per_tq - 1, 0), off + g))

    def nxt(off):
        return pl.BlockSpec((cps, HALF_STEPS, gw),
                            lambda r, g, t: (r, jnp.minimum((t + 1) * halo_per_tq, n_halo - 1), off + g))

    return pl.pallas_call(
        functools.partial(_dilated_branch_kernel, tq=tq),
        grid=(dil // cps, ngroups, l // tq),
        in_specs=[cur(qoff), prev(koff), cur(koff), nxt(koff), prev(voff), cur(voff), nxt(voff),
                  pl.BlockSpec((DIL_HG, DIL_SUB, DIL_BAND), lambda r, g, t: (g, 0, 0))],
        out_specs=[pl.BlockSpec((cps, tq, gw), lambda r, g, t: (r, t, g)),
                   pl.BlockSpec((cps, tq, LANES), lambda r, g, t: (r, t, g))],
        out_shape=[jax.ShapeDtypeStruct((dil, l, A_WIDTH), BF16),
                   jax.ShapeDtypeStruct((dil, l, ngroups * LANES), F32)],
        compiler_params=_params("parallel", "parallel", "arbitrary"),
        name=f"dilated_branch_d{dil}",
    )(*([proj_c] * 7), bias_b)


def _dilated_merge_kernel(*refs, dils):
    nb = len(dils)
    o_refs, l_refs = refs[:nb], refs[nb:2 * nb]
    fb_ref, out_ref, l_sc, o_sc = refs[2 * nb:]
    g = pl.program_id(1)
    ngroups = A_HEADS // DIL_HG
    lanes_per_head = LANES // DIL_HG
    tm = out_ref.shape[0]

    @pl.when(g < ngroups)
    def _():
        for b, dil in enumerate(dils):
            for r in range(dil):
                rows = pl.ds(r, tm // dil, stride=dil)
                l_sc[b, rows, :] = l_refs[b][r]
                for j in range(DIL_HG):
                    o_sc[b, j, rows, :] = o_refs[b][r, :, HEAD_DIM * j:HEAD_DIM * (j + 1)].astype(F32)
        ls = [l_sc[b] for b in range(nb)]
        mx = functools.reduce(jnp.maximum, ls)
        es = [jnp.exp2(l - mx) for l in ls]
        inv = 1.0 / functools.reduce(jnp.add, es)
        for j in range(DIL_HG):
            lc = slice(lanes_per_head * j, lanes_per_head * j + 1)
            acc = functools.reduce(jnp.add, [(es[b] * inv)[:, lc] * o_sc[b, j] for b in range(nb)])
            out_ref[:, HEAD_DIM * j:HEAD_DIM * (j + 1)] = acc.astype(out_ref.dtype)

    @pl.when(g == ngroups)
    def _():
        out_ref[...] = fb_ref[...]


def _dilated_merge(os_, lses, fb, *, tm=1024):
    s = fb.shape[0]
    tm = min(tm, s)
    ngroups = A_HEADS // DIL_HG
    gw = DIL_HG * HEAD_DIM
    assert fb.shape[1] == gw and HEAD_DIM == LANES
    dils = tuple(o.shape[0] for o in os_)
    clamp = lambda i, g: (0, i, jnp.minimum(g, ngroups - 1))
    return pl.pallas_call(
        functools.partial(_dilated_merge_kernel, dils=dils),
        grid=(s // tm, ngroups + 1),
        in_specs=[pl.BlockSpec((d, tm // d, gw), clamp) for d in dils]
                 + [pl.BlockSpec((d, tm // d, LANES), clamp) for d in dils]
                 + [pl.BlockSpec((tm, gw), lambda i, g: (i, 0))],
        out_specs=pl.BlockSpec((tm, gw), lambda i, g: (i, g)),
        out_shape=jax.ShapeDtypeStruct((s, A_WIDTH + B_WIDTH), BF16),
        scratch_shapes=[pltpu.VMEM((len(dils), tm, LANES), F32),
                        pltpu.VMEM((len(dils), DIL_HG, tm, LANES), F32)],
        compiler_params=_params("parallel", "arbitrary"),
        name="dilated_merge",
    )(*os_, *lses, fb)


def _dft_cos_sin(n):
    idx = np.arange(n, dtype=np.int64)
    ang = 2.0 * np.pi * ((idx[:, None] * idx[None, :]) % n) / n
    return np.cos(ang), np.sin(ang)


def _fourier_weights_kernel(cs_ref, w_ref, o_ref, *, norm):
    w = w_ref[0]
    ab = jnp.dot(cs_ref[...], w, preferred_element_type=F32, precision=lax.Precision.HIGHEST) * norm
    o_ref[0] = jnp.concatenate([ab[:HEAD_DIM], ab[HEAD_DIM:]], axis=1).astype(o_ref.dtype)


def _fourier_weights(w_f, seq):
    c, s = _dft_cos_sin(HEAD_DIM)
    cs = jnp.asarray(np.concatenate([c, s], axis=0), F32)
    norm = 1.0 / math.sqrt(seq * HEAD_DIM)
    return pl.pallas_call(
        functools.partial(_fourier_weights_kernel, norm=norm),
        grid=(B_GROUPS,),
        in_specs=[pl.BlockSpec((2 * HEAD_DIM, HEAD_DIM), lambda g: (0, 0)),
                  pl.BlockSpec((1, HEAD_DIM, HEAD_DIM), lambda g: (g, 0, 0))],
        out_specs=pl.BlockSpec((1, HEAD_DIM, 2 * HEAD_DIM), lambda g: (g, 0, 0)),
        out_shape=jax.ShapeDtypeStruct((B_GROUPS, HEAD_DIM, 2 * HEAD_DIM), BF16),
        compiler_params=_params("arbitrary"),
        name="fourier_weights",
    )(cs, w_f)


def _fourier_channel_kernel(u_ref, ab_ref, y_ref, z_ref):
    for g in range(B_GROUPS):
        hs = slice(HEAD_DIM * g, HEAD_DIM * (g + 1))
        yz = _dot(u_ref[:, hs], ab_ref[g])
        y_ref[:, hs] = yz[:, :HEAD_DIM].astype(y_ref.dtype)
        z_ref[:, hs] = yz[:, HEAD_DIM:].astype(z_ref.dtype)


def _fourier_channel(proj, ab, *, tm=1024):
    s, w = proj.shape
    ublock = (w - B_WIDTH) // B_WIDTH
    assert ublock * B_WIDTH == w - B_WIDTH
    spec = pl.BlockSpec((tm, B_WIDTH), lambda i: (i, 0))
    return pl.pallas_call(
        _fourier_channel_kernel,
        grid=(s // tm,),
        in_specs=[pl.BlockSpec((tm, B_WIDTH), lambda i: (i, ublock)),
                  pl.BlockSpec((B_GROUPS, HEAD_DIM, 2 * HEAD_DIM), lambda i: (0, 0, 0))],
        out_specs=[spec, spec],
        out_shape=[jax.ShapeDtypeStruct((s, B_WIDTH), BF16)] * 2,
        compiler_params=_params("parallel"),
        name="fourier_channel",
    )(proj, ab)


def _fourier_stage1_kernel(m1_ref, tc_ref, ts_ref, y_ref, z_ref, tre_ref, tim_ref, *, n1, n2_per_step):
    yz = jnp.concatenate([y_ref[...], z_ref[...]], axis=0)
    ab = _dot(m1_ref[...], yz)
    a, b = ab[:n1], ab[n1:]
    ch = B_WIDTH
    for q in range(n2_per_step):
        cs = slice(ch * q, ch * (q + 1))
        c = tc_ref[0, :, q:q + 1]
        s = ts_ref[0, :, q:q + 1]
        aq, bq = a[:, cs], b[:, cs]
        tre_ref[:, cs] = (aq * c + bq * s).astype(tre_ref.dtype)
        tim_ref[:, cs] = (bq * c - aq * s).astype(tim_ref.dtype)


def _fourier_stage2_kernel(m2_ref, tre_ref, tim_ref, o_ref, *, n2, k1_per_step):
    ch = B_WIDTH
    for q in range(k1_per_step):
        rs = slice(n2 * q, n2 * (q + 1))
        t = jnp.concatenate([tre_ref[rs, :], tim_ref[rs, :]], axis=0)
        o_ref[:, ch * q:ch * (q + 1)] = _dot(m2_ref[...], t).astype(o_ref.dtype)


def _fourier_position(y, z, *, n2=FFT_N2, n2_per_step=16, k1_per_step=4):
    seq, ch = y.shape
    n1 = seq // n2
    n2_per_step = min(n2_per_step, n2)
    k1_per_step = min(k1_per_step, n1)
    c1, s1 = _dft_cos_sin(n1)
    m1 = jnp.asarray(np.block([[c1, -s1], [-s1, -c1]]), BF16)
    c2, s2 = _dft_cos_sin(n2)
    m2 = jnp.asarray(np.concatenate([c2, s2], axis=1), BF16)
    k1 = np.arange(n1, dtype=np.int64)[:, None]
    nn2 = np.arange(n2, dtype=np.int64)[None, :]
    ang = 2.0 * np.pi * ((k1 * nn2) % seq) / seq
    steps = n2 // n2_per_step
    tc = jnp.asarray(np.cos(ang).reshape(n1, steps, n2_per_step).transpose(1, 0, 2), F32)
    ts = jnp.asarray(np.sin(ang).reshape(n1, steps, n2_per_step).transpose(1, 0, 2), F32)

    cols = n2_per_step * ch
    dspec = pl.BlockSpec((n1, cols), lambda t: (0, t))
    tspec = pl.BlockSpec((1, n1, n2_per_step), lambda t: (t, 0, 0))
    tre, tim = pl.pallas_call(
        functools.partial(_fourier_stage1_kernel, n1=n1, n2_per_step=n2_per_step),
        grid=(steps,),
        in_specs=[pl.BlockSpec((2 * n1, 2 * n1), lambda t: (0, 0)), tspec, tspec, dspec, dspec],
        out_specs=[dspec, dspec],
        out_shape=[jax.ShapeDtypeStruct((n1, n2 * ch), BF16)] * 2,
        compiler_params=_params("parallel"),
        name="fourier_stage1",
    )(m1, tc, ts, y.reshape(n1, n2 * ch), z.reshape(n1, n2 * ch))

    tblock = pl.BlockSpec((k1_per_step * n2, ch), lambda t: (t, 0))
    out = pl.pallas_call(
        functools.partial(_fourier_stage2_kernel, n2=n2, k1_per_step=k1_per_step),
        grid=(n1 // k1_per_step,),
        in_specs=[pl.BlockSpec((n2, 2 * n2), lambda t: (0, 0)), tblock, tblock],
        out_specs=pl.BlockSpec((n2, k1_per_step * ch), lambda t: (0, t)),
        out_shape=jax.ShapeDtypeStruct((n2, n1 * ch), BF16),
        compiler_params=_params("parallel"),
        name="fourier_stage2",
    )(m2, tre.reshape(seq, ch), tim.reshape(seq, ch))
    return out.reshape(seq, ch)


def _diff_bias_kernel(tab_ref, o_ref, *, t):
    h = pl.program_id(0)
    row = lax.broadcasted_iota(jnp.int32, (t, t), 0)
    col = lax.broadcasted_iota(jnp.int32, (t, t), 1)
    base = row - col
    for r in range(2 * DIFF_R + 1):
        d = (r - DIFF_R) * t
        o_ref[0, r] = _bias_chain(base + d, d - (t - 1), d + (t - 1),
                                  lambda b: tab_ref[b, A_HEADS + h] * LOG2_E)


def _diff_bias(table, t):
    assert DIFF_R * t - (t - 1) >= _THR[_HALF_BUCKETS - 1]
    nt = 2 * DIFF_R + 1
    return pl.pallas_call(
        functools.partial(_diff_bias_kernel, t=t),
        grid=(C_HEADS,),
        in_specs=[pl.BlockSpec(memory_space=pltpu.SMEM)],
        out_specs=pl.BlockSpec((1, nt, t, t), lambda h: (h, 0, 0, 0)),
        out_shape=jax.ShapeDtypeStruct((C_HEADS, nt, t, t), F32),
        compiler_params=_params("arbitrary"),
        name="diff_bias",
    )(table)


def _diff_attn_kernel(q_ref, k_ref, v_ref, b_ref, lam_ref, g_ref, o_ref,
                      vt_sc, s0_sc, s1_sc, p0_sc, p1_sc, alpha0_sc, alpha1_sc, m_sc, acc_sc,
                      *, t, tk, lambda_init):
    s_sc, p_sc, alpha_sc = (s0_sc, s1_sc), (p0_sc, p1_sc), (alpha0_sc, alpha1_sc)
    qg = pl.program_id(1)
    nkv = k_ref.shape[0] // tk
    ntile = q_ref.shape[0] // t
    total = ntile * nkv
    per_tile = t // tk

    @pl.when(qg == 0)
    def _():
        pad = lax.broadcasted_iota(jnp.int32, (DIFF_VT_ROWS - C_V_DIM, tk), 0)
        ones_row = jnp.where(pad == 0, 1.0, 0.0).astype(BF16)
        for j in range(nkv):
            vt_sc[j, :C_V_DIM, :] = v_ref[tk * j:tk * (j + 1), :].astype(F32).T.astype(BF16)
            vt_sc[j, C_V_DIM:, :] = ones_row

    def split(n):
        return n // nkv, n % nkv

    def key_rows(j):
        return pl.ds(pl.multiple_of(j * tk, tk), tk)

    def query_rows(tile):
        return pl.ds(pl.multiple_of(tile * t, t), t)

    def logits(n, buf):
        tile, j = split(n)
        q_tile = qg * ntile + tile
        for c in range(2):
            hs = slice(HEAD_DIM * c, HEAD_DIM * (c + 1))
            q = q_ref[query_rows(tile), hs]
            if tk <= t:
                bias_tile = jnp.clip(j // per_tile - q_tile, -DIFF_R, DIFF_R) + DIFF_R
                bias = b_ref[0, bias_tile, pl.ds(pl.multiple_of((j % per_tile) * tk, tk), tk), :]
                s_sc[buf][c] = _dot_nt(k_ref[key_rows(j), hs], q) + bias
            else:
                for r in range(tk // t):
                    bias_tile = jnp.clip(j * (tk // t) + r - q_tile, -DIFF_R, DIFF_R) + DIFF_R
                    k_rows = pl.ds(pl.multiple_of(j * tk + r * t, t), t)
                    s_sc[buf][c, t * r:t * (r + 1), :] = _dot_nt(k_ref[k_rows, hs], q) + b_ref[0, bias_tile]

    def softmax(n, buf):
        _, j = split(n)
        for c in range(2):
            m_cur = jnp.max(s_sc[buf][c], axis=0, keepdims=True)
            m_prev = jnp.where(j == 0, NEG_INF, m_sc[c])
            m_next = jnp.maximum(m_prev, m_cur)
            alpha_sc[buf][c] = jnp.exp2(m_prev - m_next)
            m_sc[c] = m_next
            p_sc[buf][c] = jnp.exp2(s_sc[buf][c] - m_next).astype(BF16)

    def values(n, buf):
        _, j = split(n)
        for c in range(2):
            acc_sc[c] = acc_sc[c] * alpha_sc[buf][c] + _dot(vt_sc[j], p_sc[buf][c])

    def finalize(tile):
        lp = lam_ref[...]
        lam = (jnp.exp(jnp.sum(lp[0:1] * lp[1:2], axis=-1, keepdims=True))
               - jnp.exp(jnp.sum(lp[2:3] * lp[3:4], axis=-1, keepdims=True)) + lambda_init)
        num0, den0 = acc_sc[0, :C_V_DIM, :], acc_sc[0, C_V_DIM:C_V_DIM + 1, :]
        num1, den1 = acc_sc[1, :C_V_DIM, :], acc_sc[1, C_V_DIM:C_V_DIM + 1, :]
        o = num0 * (1.0 / den0) - lam * (num1 * (1.0 / den1))
        ms = jnp.mean(o * o, axis=0, keepdims=True)
        o = o * lax.rsqrt(ms + NORM_EPS) * (g_ref[...] * (1.0 - lambda_init))
        o_ref[query_rows(tile), :] = o.T.astype(o_ref.dtype)

    m_sc[...] = jnp.full(m_sc.shape, NEG_INF, F32)
    acc_sc[...] = jnp.zeros(acc_sc.shape, F32)

    logits(0, 0)
    logits(1, 1)
    softmax(0, 0)

    def body(i, carry):
        n = 2 * i + 1
        logits(n + 1, 0)
        softmax(n, 1)
        values(n - 1, 0)
        logits(n + 2, 1)
        softmax(n + 1, 0)
        values(n, 1)

        @pl.when((n + 1) % nkv == 0)
        def _():
            finalize(n // nkv)

        return carry

    lax.fori_loop(0, total // 2 - 1, body, 0)
    softmax(total - 1, 1)
    values(total - 2, 0)
    values(total - 1, 1)
    finalize(ntile - 1)


def _diff_attn(proj, bias, lam_params, subln_g, lambda_init, *, t, tk=DIFF_TK, tiles_per_step=DIFF_TILES_PER_STEP):
    s = proj.shape[0]
    nt = 2 * DIFF_R + 1
    tk = min(tk, s // 4)
    nkv = s // tk
    tiles_per_step = min(tiles_per_step, s // t)
    tq = tiles_per_step * t
    assert nkv * tk == s and nkv % 2 == 0 and nkv >= 4 and (t % tk == 0 or tk % t == 0) and s % tq == 0
    return pl.pallas_call(
        functools.partial(_diff_attn_kernel, t=t, tk=tk, lambda_init=lambda_init),
        grid=(C_HEADS, s // tq),
        in_specs=[pl.BlockSpec((tq, C_V_DIM), lambda h, i: (i, h)),
                  pl.BlockSpec((s, C_V_DIM), lambda h, i: (0, C_HEADS + h)),
                  pl.BlockSpec((s, C_V_DIM), lambda h, i: (0, 2 * C_HEADS + h)),
                  pl.BlockSpec((1, nt, t, t), lambda h, i: (h, 0, 0, 0)),
                  pl.BlockSpec((4, HEAD_DIM), lambda h, i: (0, 0)),
                  pl.BlockSpec((C_V_DIM, 1), lambda h, i: (0, 0))],
        out_specs=pl.BlockSpec((tq, C_V_DIM), lambda h, i: (i, h)),
        out_shape=jax.ShapeDtypeStruct((s, C_HEADS * C_V_DIM), BF16),
        scratch_shapes=[pltpu.VMEM((nkv, DIFF_VT_ROWS, tk), BF16),
                        pltpu.VMEM((2, tk, t), F32), pltpu.VMEM((2, tk, t), F32),
                        pltpu.VMEM((2, tk, t), BF16), pltpu.VMEM((2, tk, t), BF16),
                        pltpu.VMEM((2, 1, t), F32), pltpu.VMEM((2, 1, t), F32), pltpu.VMEM((2, 1, t), F32),
                        pltpu.VMEM((2, DIFF_VT_ROWS, t), F32)],
        compiler_params=_params("arbitrary", "arbitrary"),
        name="diff_attn",
    )(proj, proj, proj, bias, lam_params, subln_g.reshape(C_V_DIM, 1))


def kernel(x, norm_mix_g, norm_ffn_g, norm_final_g, rel_bias_table, w_in_even, w_fnet, w_out_even,
           w_qkv_odd, lambda_q1, lambda_k1, lambda_q2, lambda_k2, subln_g, w_out_odd, w_ff1, w_ff2):
    batch, seq, d_model = x.shape
    depth = norm_mix_g.shape[0]
    table = rel_bias_table.astype(F32)
    dil_bias = _dilated_bias(table)
    diff_t = min(DIFF_T, seq)
    diff_bias = _diff_bias(table, diff_t)
    c_qk_width = C_HEADS * 2 * HEAD_DIM
    odd_scale = jnp.concatenate([jnp.full((1, c_qk_width), LOG2_E / math.sqrt(HEAD_DIM), F32),
                                 jnp.ones((1, w_qkv_odd.shape[2] - c_qk_width), F32)], axis=1)
    even_scale = jnp.concatenate([jnp.full((1, A_WIDTH), LOG2_E / math.sqrt(HEAD_DIM), F32),
                                  jnp.ones((1, w_in_even.shape[2] - A_WIDTH), F32)], axis=1)

    w_in_even_b, w_qkv_odd_b = w_in_even.astype(BF16), w_qkv_odd.astype(BF16)
    w_ff1_b, w_ff2_b = w_ff1.astype(BF16), w_ff2.astype(BF16)
    w_out_even_f, w_out_odd_f = w_out_even.astype(F32), w_out_odd.astype(F32)

    outs = []
    for bidx in range(batch):
        xs = x.reshape(seq, d_model) if batch == 1 else x[bidx]
        for i in range(depth):
            j = i // 2
            if i % 2 == 0:
                class_dils = tuple(dil for _, dil in DILATED_CONFIGS if dil > 1)
                proj, *by_class = _norm_matmul(xs, norm_mix_g[i], w_in_even_b, j, even_scale,
                                               class_dils=class_dils)
                by_class = dict(zip(class_dils, by_class))
                by_class[1] = proj.reshape(1, *proj.shape)
                branches = [_dilated_branch(by_class[dil], dil_bias[bi])
                            for bi, (_, dil) in enumerate(DILATED_CONFIGS)]
                y, z = _fourier_channel(proj, _fourier_weights(w_fnet[j], seq))
                fb = _fourier_position(y, z)
                mixed = _dilated_merge([b[0] for b in branches], [b[1] for b in branches], fb)
                xs = _matmul_residual(mixed, w_out_even_f, j, xs)
            else:
                lambda_init = 0.8 - 0.6 * math.exp(-0.3 * i)
                proj, = _norm_matmul(xs, norm_mix_g[i], w_qkv_odd_b, j, odd_scale)
                lam_params = jnp.stack([lambda_q1[j], lambda_k1[j], lambda_q2[j], lambda_k2[j]]).astype(F32)
                attn = _diff_attn(proj, diff_bias, lam_params, subln_g[j], lambda_init, t=diff_t)
                xs = _matmul_residual(attn, w_out_odd_f, j, xs)
            xs = _mlp(xs, norm_ffn_g[i], w_ff1_b, w_ff2_b, i, norm_final_g, final_norm=(i == depth - 1))
        outs.append(xs)
    return outs[0].reshape(1, seq, d_model) if batch == 1 else jnp.stack(outs, axis=0)
```
